```python
import jax, jax.numpy as jnp
from jax import lax
import numpy as np

D_MODEL = 1024
BATCH = 2
SEQ = 16384
DEPTH = 4

CHUNK = 64
HEAD_DIM = 64
RET_W = D_MODEL // 2
SB_W = D_MODEL // 4
RWKV_W = D_MODEL - RET_W - SB_W
RET_HEADS = RET_W // HEAD_DIM
SB_HEADS = SB_W // HEAD_DIM
RWKV_HEADS = RWKV_W // HEAD_DIM
MIX_W = RET_W + SB_W + RWKV_W
DECAY_LORA = 64
AAA_LORA = 64
GATE_LORA = 128
RWKV_IN = 3 * RWKV_W + DECAY_LORA + AAA_LORA + GATE_LORA
RET_IN = 4 * RET_W
SB_IN = 3 * SB_W
IN_W = RET_IN + SB_IN + RWKV_IN
D_FF = 4 * D_MODEL
PLE_DIM = 256
SB_BLOCK = 128
ROPE_BASE = 10000.0
NORM_EPS = 1e-6
RWKV_GN_EPS = 64e-5

kernel_name = "hybrid_retention_stickbreaking_rwkv7_trunk"

F32 = jnp.float32


def rms_norm(x, g):
    xf = x.astype(F32)
    y = xf * lax.rsqrt(jnp.mean(xf * xf, axis=-1, keepdims=True) + NORM_EPS)
    return (y * g.astype(F32)).astype(x.dtype)


def rope(t, cos, sin):
    t1, t2 = jnp.split(t, 2, axis=-1)
    return jnp.concatenate([t1 * cos - t2 * sin, t1 * sin + t2 * cos], axis=-1)


def retention(q, k, v, g, norm_g):
    B, S, _ = q.shape
    H, Dh, L = RET_HEADS, HEAD_DIM, CHUNK
    C = S // L
    q = q.astype(F32).reshape(B, S, H, Dh)
    k = k.astype(F32).reshape(B, S, H, Dh)
    v = v.astype(F32).reshape(B, S, H, Dh)
    pos = jnp.arange(S, dtype=F32)
    inv_freq = 1.0 / (ROPE_BASE ** jnp.linspace(0.0, 1.0, Dh // 2, dtype=F32))
    ang = pos[:, None] * inv_freq[None, :]
    cos = jnp.cos(ang)[:, None, :]
    sin = jnp.sin(ang)[:, None, :]
    q = rope(q, cos, sin)
    k = rope(k, cos, sin) * (Dh ** -0.5)
    log_g = jnp.log(1.0 - jnp.exp2(-5.0 - jnp.arange(H, dtype=F32)))
    i = jnp.arange(L, dtype=F32)
    intra_decay = jnp.exp(jnp.abs(i[:, None] - i[None, :])[None] * log_g[:, None, None])
    qc = q.reshape(B, C, L, H, Dh)
    kc = k.reshape(B, C, L, H, Dh)
    vc = v.reshape(B, C, L, H, Dh)
    scores = jnp.einsum('bcihd,bcjhd->bchij', qc, kc) * intra_decay
    o_intra = jnp.einsum('bchij,bcjhd->bcihd', scores, vc)
    k_dec = jnp.exp((L - 1 - i)[:, None] * log_g[None, :])
    kv = jnp.einsum('bcjhd,bcjhe->cbhde', kc * k_dec[:, :, None], vc)
    chunk_decay = jnp.exp(L * log_g)[None, :, None, None]

    def step(state, kv_c):
        return state * chunk_decay + kv_c, state

    _, r_prev = lax.scan(step, jnp.zeros((B, H, Dh, Dh), F32), kv)
    q_dec = jnp.exp((i + 1)[:, None] * log_g[None, :])
    o_cross = jnp.einsum('bcihd,cbhde->bcihe', qc * q_dec[:, :, None], r_prev)
    o = (o_intra + o_cross).reshape(B, S, H, Dh)
    o = o * lax.rsqrt(jnp.mean(o * o, axis=-1, keepdims=True) + NORM_EPS)
    o = o.reshape(B, S, RET_W) * norm_g.astype(F32)
    return jax.nn.silu(g.astype(F32)) * o


def stick_breaking(q, k, v):
    B, S, _ = q.shape
    H, Dh, T = SB_HEADS, HEAD_DIM, SB_BLOCK
    nb = S // T
    qh = q.astype(F32).reshape(B, S, H, Dh).transpose(0, 2, 1, 3) * (Dh ** -0.5)
    kh = k.astype(F32).reshape(B, S, H, Dh).transpose(0, 2, 1, 3)
    vh = v.astype(F32).reshape(B, S, H, Dh).transpose(0, 2, 1, 3)
    strict = jnp.asarray(np.tril(np.ones((T, T), dtype=bool), -1))
    rev_incl = jnp.asarray(np.tril(np.ones((T, T), dtype=np.float32)))
    outs = []
    for qi in range(nb):
        lo, hi = qi * T, (qi + 1) * T
        qb = qh[:, :, lo:hi]
        zd = jnp.einsum('bhtd,bhsd->bhts', qb, kh[:, :, lo:hi])
        lsd = jnp.where(strict, -jax.nn.softplus(zd), 0.0)
        cumd = jnp.einsum('bhtj,js->bhts', lsd, rev_incl)
        wd = jnp.where(strict, jnp.exp(zd + cumd), 0.0)
        o = jnp.einsum('bhts,bhsd->bhtd', wd, vh[:, :, lo:hi])
        if qi > 0:
            tot_d = jnp.sum(lsd, axis=-1)
            kp = kh[:, :, :lo].reshape(B, H, qi, T, Dh)
            vp = vh[:, :, :lo].reshape(B, H, qi, T, Dh)
            zp = jnp.einsum('bhtd,bhnsd->bhtns', qb, kp)
            lsp = -jax.nn.softplus(zp)
            tot_p = jnp.sum(lsp, axis=-1)
            later_blocks = jnp.asarray(np.tril(np.ones((qi, qi), dtype=np.float32), -1))
            after = jnp.einsum('bhtm,mn->bhtn', tot_p, later_blocks) + tot_d[..., None]
            cump = jnp.einsum('bhtnj,js->bhtns', lsp, rev_incl) + after[..., None]
            wp = jnp.exp(zp + cump)
            o = o + jnp.einsum('bhtns,bhnsd->bhtd', wp, vp)
        outs.append(o)
    o = jnp.concatenate(outs, axis=2)
    return o.transpose(0, 2, 1, 3).reshape(B, S, SB_W)


def rwkv7(z, mu, w0, w_up, a0, a_up, g_up, k_k, k_a, r_k, ln_w, ln_b):
    B, S, _ = z.shape
    H, N, W = RWKV_HEADS, HEAD_DIM, RWKV_W
    z = z.astype(F32)
    z_prev = jnp.pad(z, ((0, 0), (1, 0), (0, 0)))[:, :-1]
    z = z + mu.astype(F32) * (z_prev - z)
    r, k, v, wd, ad, gd = jnp.split(
        z, [W, 2 * W, 3 * W, 3 * W + DECAY_LORA, 3 * W + DECAY_LORA + AAA_LORA], axis=-1)
    log_w = -jax.nn.softplus(-(w0.astype(F32) + jnp.tanh(wd) @ w_up.astype(F32))) - 0.5
    decay = jnp.exp(-jnp.exp(log_w))
    a = jax.nn.sigmoid(a0.astype(F32) + ad @ a_up.astype(F32))
    g = jax.nn.sigmoid(gd) @ g_up.astype(F32)

    def heads(t):
        return t.reshape(B, S, H, N)

    kk = heads(k * k_k.astype(F32))
    kk = kk / jnp.maximum(jnp.sqrt(jnp.sum(kk * kk, axis=-1, keepdims=True)), 1e-12)
    k_mod = heads(k * (1.0 + (a - 1.0) * k_a.astype(F32)))
    r_h, v_h, w_h, a_h = heads(r), heads(v), heads(decay), heads(a)
    xs = tuple(t.transpose(1, 0, 2, 3) for t in (r_h, w_h, k_mod, v_h, -kk, kk * a_h))

    def step(state, inp):
        rt, wt, kt, vt, at, bt = inp
        sa = jnp.einsum('bhvk,bhk->bhv', state, at)
        state = state * wt[:, :, None, :] + sa[..., None] * bt[:, :, None, :] + vt[..., None] * kt[:, :, None, :]
        return state, jnp.einsum('bhvk,bhk->bhv', state, rt)

    _, ys = lax.scan(step, jnp.zeros((B, H, N, N), F32), xs)
    y = ys.transpose(1, 0, 2, 3)
    mean = jnp.mean(y, axis=-1, keepdims=True)
    var = jnp.mean(jnp.square(y - mean), axis=-1, keepdims=True)
    y = ((y - mean) * lax.rsqrt(var + RWKV_GN_EPS)).reshape(B, S, W)
    y = y * ln_w.astype(F32) + ln_b.astype(F32)
    bonus = jnp.sum(r_h * k_mod * r_k.astype(F32).reshape(H, N), axis=-1, keepdims=True) * v_h
    y = y + bonus.reshape(B, S, W)
    return y * g


def setup_inputs(seed: int = 0) -> dict:
    key = jax.random.key(seed)
    ks = jax.random.split(key, 26)
    nrm = lambda k, shape, s: jax.random.normal(k, shape, F32) * s
    ramp = -6.0 + 5.0 * (jnp.arange(HEAD_DIM, dtype=F32) / (HEAD_DIM - 1))
    w0_base = jnp.tile(ramp, RWKV_HEADS)[None, :]
    return {
        "x": nrm(ks[0], (BATCH, SEQ, D_MODEL), 1.0),
        "p": nrm(ks[1], (DEPTH, BATCH, SEQ, PLE_DIM), 1.0),
        "norm_mix_g": 1.0 + nrm(ks[2], (DEPTH, D_MODEL), 0.02),
        "norm_mlp_g": 1.0 + nrm(ks[3], (DEPTH, D_MODEL), 0.02),
        "norm_ple_g": 1.0 + nrm(ks[4], (DEPTH, D_MODEL), 0.02),
        "w_in": nrm(ks[5], (DEPTH, D_MODEL, IN_W), D_MODEL ** -0.5),
        "ret_norm_g": 1.0 + nrm(ks[6], (DEPTH, RET_W), 0.02),
        "rwkv_mu": jax.random.uniform(ks[7], (DEPTH, RWKV_IN), F32),
        "rwkv_w0": w0_base + nrm(ks[8], (DEPTH, RWKV_W), 0.1),
        "rwkv_w_up": nrm(ks[9], (DEPTH, DECAY_LORA, RWKV_W), 0.5 * DECAY_LORA ** -0.5),
        "rwkv_a0": nrm(ks[10], (DEPTH, RWKV_W), 0.1),
        "rwkv_a_up": nrm(ks[11], (DEPTH, AAA_LORA, RWKV_W), AAA_LORA ** -0.5),
        "rwkv_g_up": nrm(ks[12], (DEPTH, GATE_LORA, RWKV_W), GATE_LORA ** -0.5),
        "rwkv_k_k": 0.85 + nrm(ks[13], (DEPTH, RWKV_W), 0.02),
        "rwkv_k_a": 1.0 + nrm(ks[14], (DEPTH, RWKV_W), 0.02),
        "rwkv_r_k": nrm(ks[15], (DEPTH, RWKV_W), 0.1),
        "rwkv_ln_w": 1.0 + nrm(ks[16], (DEPTH, RWKV_W), 0.02),
        "rwkv_ln_b": nrm(ks[17], (DEPTH, RWKV_W), 0.02),
        "w_o": nrm(ks[18], (DEPTH, MIX_W, D_MODEL), MIX_W ** -0.5),
        "w_mlp_in": nrm(ks[19], (DEPTH, D_MODEL, D_FF), D_MODEL ** -0.5),
        "w_mlp_out": nrm(ks[20], (DEPTH, D_FF, D_MODEL), 0.5 * D_FF ** -0.5),
        "w_pe": nrm(ks[21], (DEPTH, PLE_DIM, D_MODEL), PLE_DIM ** -0.5),
        "w_pg": nrm(ks[22], (DEPTH, D_MODEL, D_MODEL), D_MODEL ** -0.5),
        "final_norm_g": 1.0 + nrm(ks[23], (D_MODEL,), 0.02),
    }


def reference(x, p, norm_mix_g, norm_mlp_g, norm_ple_g, w_in, ret_norm_g, rwkv_mu, rwkv_w0,
              rwkv_w_up, rwkv_a0, rwkv_a_up, rwkv_g_up, rwkv_k_k, rwkv_k_a, rwkv_r_k,
              rwkv_ln_w, rwkv_ln_b, w_o, w_mlp_in, w_mlp_out, w_pe, w_pg, final_norm_g):
    h = x
    for i in range(DEPTH):
        n = rms_norm(h, norm_mix_g[i])
        proj = n @ w_in[i]
        ret_in, sb_in, rwkv_in = jnp.split(proj, [RET_IN, RET_IN + SB_IN], axis=-1)
        rq, rk, rv, rg = jnp.split(ret_in, 4, axis=-1)
        sq, sk, sv = jnp.split(sb_in, 3, axis=-1)
        o_ret = retention(rq, rk, rv, rg, ret_norm_g[i])
        o_sb = stick_breaking(sq, sk, sv)
        o_rwkv = rwkv7(rwkv_in, rwkv_mu[i], rwkv_w0[i], rwkv_w_up[i], rwkv_a0[i], rwkv_a_up[i],
                       rwkv_g_up[i], rwkv_k_k[i], rwkv_k_a[i], rwkv_r_k[i], rwkv_ln_w[i], rwkv_ln_b[i])
        mix = jnp.concatenate([o_ret, o_sb, o_rwkv], axis=-1).astype(h.dtype)
        h = h + mix @ w_o[i]
        n = rms_norm(h, norm_mlp_g[i])
        h = h + jnp.square(jax.nn.relu(n @ w_mlp_in[i])) @ w_mlp_out[i]
        n = rms_norm(h, norm_ple_g[i])
        h = h + jax.nn.sigmoid(n @ w_pg[i]) * (p[i] @ w_pe[i])
    return rms_norm(h, final_norm_g)
```

```python
import functools
import math

import numpy as np
import jax
import jax.numpy as jnp
from jax import lax
from jax.experimental import pallas as pl
from jax.experimental.pallas import tpu as pltpu

F32 = jnp.float32
BF16 = jnp.bfloat16

D_MODEL = 1024
HEAD_DIM = 64
RET_W = 512
SB_W = 256
RWKV_W = 256
RET_HEADS = 8
SB_HEADS = 4
RWKV_HEADS = 4
DECAY_LORA = 64
AAA_LORA = 64
GATE_LORA = 128
RWKV_IN = 3 * RWKV_W + DECAY_LORA + AAA_LORA + GATE_LORA
RET_IN = 4 * RET_W
SB_IN = 3 * SB_W
IN_W = RET_IN + SB_IN + RWKV_IN
D_FF = 4 * D_MODEL
PLE_DIM = 256
RET_CHUNK = 64
SB_BLOCK = 128
ROPE_BASE = 10000.0
NORM_EPS = 1e-6
RWKV_GN_EPS = 64e-5

COL_RET = 0
COL_RWKV = RET_IN
COL_SBQ = COL_RWKV + RWKV_IN
COL_SBKV = COL_SBQ + SB_W

RET_LOG_G = tuple(math.log(1.0 - 2.0 ** (-5.0 - h)) for h in range(RET_HEADS))
SB_SKIP_LOG = -104.0
RWKV_CHUNK = 64

VMEM_LIMIT = 56 * 1024 * 1024

NT = (((1,), (1,)), ((), ()))
TN = (((0,), (0,)), ((), ()))


def _mm(a, b, dims=None, exact=False):
    if dims is None:
        dims = (((a.ndim - 1,), (0,)), ((), ()))
    if exact:
        return lax.dot_general(a, b, dims, precision=lax.Precision.HIGHEST, preferred_element_type=F32)
    return lax.dot_general(a.astype(BF16), b.astype(BF16), dims, preferred_element_type=F32)


def _rms(x, g):
    return x * lax.rsqrt(jnp.mean(x * x, axis=-1, keepdims=True) + NORM_EPS) * g


def _softplus(z):
    return jnp.maximum(z, 0.0) + jnp.log(1.0 + jnp.exp(-jnp.abs(z)))


def _sigmoid(z):
    return 1.0 / (1.0 + jnp.exp(-z))


def _norm_proj_kernel(h_ref, g_ref, w_ref, o_ref):
    n = _rms(h_ref[...], g_ref[...])
    o_ref[...] = jnp.dot(n.astype(BF16), w_ref[...], preferred_element_type=F32)


def _norm_proj(h, g, w, tm=512):
    M, D = h.shape
    N = w.shape[1]
    return pl.pallas_call(
        _norm_proj_kernel,
        out_shape=jax.ShapeDtypeStruct((M, N), F32),
        grid=(M // tm,),
        in_specs=[
            pl.BlockSpec((tm, D), lambda i: (i, 0)),
            pl.BlockSpec((1, D), lambda i: (0, 0)),
            pl.BlockSpec((D, N), lambda i: (0, 0)),
        ],
        out_specs=pl.BlockSpec((tm, N), lambda i: (i, 0)),
        compiler_params=pltpu.CompilerParams(
            dimension_semantics=("parallel",), vmem_limit_bytes=VMEM_LIMIT),
        name="norm_proj",
    )(h, g.reshape(1, D), w)


def _retention_kernel(q_ref, k_ref, v_ref, g_ref, cos_ref, sin_ref, ng_ref, o_ref, state_ref, *, T):
    @pl.when(pl.program_id(1) == 0)
    def _():
        state_ref[...] = jnp.zeros_like(state_ref)

    cos = jnp.concatenate([cos_ref[...]] * 4, axis=-1)
    sin = jnp.concatenate([sin_ref[...]] * 4, axis=-1)
    lane = lax.broadcasted_iota(jnp.int32, (T, RET_W), 1)
    first_half = (lane % HEAD_DIM) < (HEAD_DIM // 2)

    def rope(t):
        swapped = jnp.where(first_half, pltpu.roll(t, RET_W - HEAD_DIM // 2, 1), pltpu.roll(t, HEAD_DIM // 2, 1))
        return t * cos + swapped * sin

    q = rope(q_ref[0])
    k = rope(k_ref[0]) * (HEAD_DIM ** -0.5)
    v = v_ref[0]
    gate = g_ref[0]
    gate = gate * _sigmoid(gate)
    ng = ng_ref[...]

    n_idx = lax.broadcasted_iota(jnp.int32, (T, T), 0)
    m_idx = lax.broadcasted_iota(jnp.int32, (T, T), 1)
    visible = (m_idx <= n_idx) | ((m_idx // RET_CHUNK) == (n_idx // RET_CHUNK))
    dist = jnp.abs(n_idx - m_idx).astype(F32)
    pos = lax.broadcasted_iota(jnp.int32, (T, 1), 0).astype(F32)

    for h in range(RET_HEADS):
        lg = RET_LOG_G[h]
        sl = slice(h * HEAD_DIM, (h + 1) * HEAD_DIM)
        qh, kh, vh = q[:, sl], k[:, sl], v[:, sl]
        decay = jnp.where(visible, jnp.exp(dist * lg), 0.0)
        scores = _mm(qh, kh, NT) * decay
        st = state_ref[h]
        o = _mm(scores, vh) + _mm(qh * jnp.exp((pos + 1.0) * lg), st)
        state_ref[h] = st * math.exp(T * lg) + _mm(kh * jnp.exp((T - 1.0 - pos) * lg), vh, TN)
        o = o * lax.rsqrt(jnp.mean(o * o, axis=-1, keepdims=True) + NORM_EPS)
        o_ref[0, :, sl] = (o * ng[:, sl] * gate[:, sl]).astype(o_ref.dtype)


def _retention(proj, cos4, sin4, norm_g, T=256):
    B, S, _ = proj.shape
    blk = lambda c: pl.BlockSpec((1, T, RET_W), lambda b, s, c=c: (b, s, c))
    return pl.pallas_call(
        functools.partial(_retention_kernel, T=T),
        out_shape=jax.ShapeDtypeStruct((B, S, RET_W), BF16),
        grid=(B, S // T),
        in_specs=[
            blk(0), blk(1), blk(2), blk(3),
            pl.BlockSpec((T, 128), lambda b, s: (s, 0)),
            pl.BlockSpec((T, 128), lambda b, s: (s, 0)),
            pl.BlockSpec((1, RET_W), lambda b, s: (0, 0)),
        ],
        out_specs=pl.BlockSpec((1, T, RET_W), lambda b, s: (b, s, 0)),
        scratch_shapes=[pltpu.VMEM((RET_HEADS, HEAD_DIM, HEAD_DIM), F32)],
        compiler_params=pltpu.CompilerParams(
            dimension_semantics=("parallel", "arbitrary"), vmem_limit_bytes=VMEM_LIMIT),
        name="retention",
    )(proj, proj, proj, proj, cos4, sin4, norm_g.reshape(1, RET_W))


def _stick_kernel(q_ref, kv_ref, o_ref, *, TB):
    i = pl.program_id(2)
    row = lax.broadcasted_iota(jnp.int32, (TB, TB), 0)
    col = lax.broadcasted_iota(jnp.int32, (TB, TB), 1)
    strict = col < row
    rev_incl = (row >= col).astype(F32)

    for hh in range(2):
        q = q_ref[0, :, hh * HEAD_DIM:(hh + 1) * HEAD_DIM] * (HEAD_DIM ** -0.5)
        kcol = 2 * hh * HEAD_DIM

        def load_kv(n, kcol=kcol):
            start = pl.multiple_of(n * TB, TB)
            kb = kv_ref[0, pl.ds(start, TB), kcol:kcol + HEAD_DIM]
            vb = kv_ref[0, pl.ds(start, TB), kcol + HEAD_DIM:kcol + 2 * HEAD_DIM]
            return kb, vb

        kb, vb = load_kv(i)
        z = _mm(q, kb, NT, exact=True)
        ls = jnp.where(strict, -_softplus(z), 0.0)
        cum = _mm(ls, rev_incl, exact=True)
        w = jnp.where(strict, jnp.exp(z + cum), 0.0)
        o = _mm(w, vb)
        acc = jnp.sum(ls, axis=-1, keepdims=True)

        def cond(c):
            n, _, _, top = c
            return jnp.logical_and(n >= 0, top > SB_SKIP_LOG)

        def body(c, q=q, load_kv=load_kv):
            n, o, acc, _ = c
            kb, vb = load_kv(n)
            z = _mm(q, kb, NT, exact=True)
            ls = -_softplus(z)
            cum = _mm(ls, rev_incl, exact=True) + acc
            o = o + _mm(jnp.exp(z + cum), vb)
            acc = acc + jnp.sum(ls, axis=-1, keepdims=True)
            return n - 1, o, acc, jnp.max(acc)

        _, o, _, _ = lax.while_loop(cond, body, (i - 1, o, acc, jnp.max(acc)))
        o_ref[0, :, hh * HEAD_DIM:(hh + 1) * HEAD_DIM] = o.astype(o_ref.dtype)


def _stick_breaking(proj, TB=SB_BLOCK):
    B, S, _ = proj.shape
    q_blk0 = COL_SBQ // 128
    kv_blk0 = COL_SBKV // 256
    return pl.pallas_call(
        functools.partial(_stick_kernel, TB=TB),
        out_shape=jax.ShapeDtypeStruct((B, S, SB_W), BF16),
        grid=(B, SB_HEADS // 2, S // TB),
        in_specs=[
            pl.BlockSpec((1, TB, 128), lambda b, j, i: (b, i, q_blk0 + j)),
            pl.BlockSpec((1, S, 256), lambda b, j, i: (b, 0, kv_blk0 + j)),
        ],
        out_specs=pl.BlockSpec((1, TB, 128), lambda b, j, i: (b, i, j)),
        compiler_params=pltpu.CompilerParams(
            dimension_semantics=("parallel", "parallel", "arbitrary"), vmem_limit_bytes=VMEM_LIMIT),
        name="stick_breaking",
    )(proj, proj)


def _rwkv_kernel(z_ref, mu_ref, w0_ref, wup_ref, a0_ref, aup_ref, gup_ref, kk_ref, ka_ref, rk_ref,
                 lnw_ref, lnb_ref, o_ref,
                 state_ref, prev_ref, lw_s, a_s, b_s, r_s, k_s, v_s, y_s, *, TS):
    C = RWKV_CHUNK
    W = RWKV_W

    @pl.when(pl.program_id(1) == 0)
    def _():
        state_ref[...] = jnp.zeros_like(state_ref)
        prev_ref[...] = jnp.zeros_like(prev_ref)

    z = z_ref[0]
    first_row = lax.broadcasted_iota(jnp.int32, (TS, 1), 0) == 0
    z_prev = jnp.where(first_row, prev_ref[...], pltpu.roll(z, 1, 0))
    prev_ref[...] = z[TS - 1:TS, :]
    z = z + mu_ref[...] * (z_prev - z)

    r, k, v = z[:, 0:W], z[:, W:2 * W], z[:, 2 * W:3 * W]
    wd = z[:, 3 * W:3 * W + DECAY_LORA]
    ad = z[:, 3 * W + DECAY_LORA:3 * W + DECAY_LORA + AAA_LORA]
    gd = z[:, 3 * W + DECAY_LORA + AAA_LORA:]

    gi = lax.broadcasted_iota(jnp.int32, (W, W), 0) // HEAD_DIM
    gj = lax.broadcasted_iota(jnp.int32, (W, W), 1) // HEAD_DIM
    same_head = (gi == gj).astype(F32)

    def head_sum(x):
        return _mm(x, same_head, exact=True)

    log_w = -_softplus(-(w0_ref[...] + _mm(jnp.tanh(wd), wup_ref[...], exact=True))) - 0.5
    lw = -jnp.exp(log_w)
    a = _sigmoid(a0_ref[...] + _mm(ad, aup_ref[...], exact=True))
    g = _mm(_sigmoid(gd), gup_ref[...], exact=True)
    kk = k * kk_ref[...]
    kk = kk / jnp.maximum(jnp.sqrt(head_sum(kk * kk)), 1e-12)
    k_mod = k * (1.0 + (a - 1.0) * ka_ref[...])

    lw_s[...] = lw
    a_s[...] = -kk
    b_s[...] = kk * a
    r_s[...] = r
    k_s[...] = k_mod
    v_s[...] = v

    ti = lax.broadcasted_iota(jnp.int32, (C, C), 0)
    tj = lax.broadcasted_iota(jnp.int32, (C, C), 1)
    strict = tj < ti
    incl = tj <= ti
    tri_incl = incl.astype(F32)
    eye = (ti == tj).astype(F32)

    def chunk(c, carry):
        r0 = pl.multiple_of(c * C, C)
        rows = pl.ds(r0, C)
        lw_c = lw_s[rows, :]
        cs = _mm(tri_incl, lw_c, exact=True)
        e_incl = jnp.exp(cs)
        e_excl = jnp.exp(cs - lw_c)
        e_inv = jnp.exp(-cs)
        p_end = e_incl[C - 1:C, :]
        at = a_s[rows, :] * e_excl
        rt = r_s[rows, :] * e_incl
        bt = b_s[rows, :] * e_inv
        kt = k_s[rows, :] * e_inv
        bh = bt * p_end
        kh = kt * p_end
        vc = v_s[rows, :]
        for h in range(RWKV_HEADS):
            sl = slice(h * HEAD_DIM, (h + 1) * HEAD_DIM)
            x = jnp.concatenate([at[:, sl], rt[:, sl]], axis=0)
            y = jnp.concatenate([bt[:, sl], kt[:, sl]], axis=0)
            gram = _mm(x, y, NT, exact=True)
            a_ab = jnp.where(strict, gram[0:C, 0:C], 0.0)
            a_ak = jnp.where(strict, gram[0:C, C:2 * C], 0.0)
            m_rb = jnp.where(incl, gram[C:2 * C, 0:C], 0.0)
            m_rk = jnp.where(incl, gram[C:2 * C, C:2 * C], 0.0)
            inv = eye + a_ab
            pw = a_ab
            for _ in range(int(math.log2(C)) - 1):
                pw = _mm(pw, pw, exact=True)
                inv = inv + _mm(pw, inv, exact=True)
            vh = vc[:, sl]
            st = state_ref[h]
            rhs = jnp.concatenate([at[:, sl], _mm(a_ak, vh, exact=True)], axis=1)
            sol = _mm(inv, rhs, exact=True)
            u = _mm(sol[:, 0:HEAD_DIM], st, NT, exact=True) + sol[:, HEAD_DIM:]
            yh = _mm(rt[:, sl], st, NT, exact=True) + _mm(m_rb, u, exact=True) + _mm(m_rk, vh, exact=True)
            state_ref[h] = (st * p_end[:, sl] + _mm(u, bh[:, sl], TN, exact=True)
                            + _mm(vh, kh[:, sl], TN, exact=True))
            y_s[rows, sl] = yh
        return carry

    lax.fori_loop(0, TS // C, chunk, 0)

    y = y_s[...]
    mean = head_sum(y) * (1.0 / HEAD_DIM)
    d = y - mean
    var = head_sum(d * d) * (1.0 / HEAD_DIM)
    y = d * lax.rsqrt(var + RWKV_GN_EPS) * lnw_ref[...] + lnb_ref[...]
    bonus = head_sum(r * k_mod * rk_ref[...]) * v
    o_ref[0] = ((y + bonus) * g).astype(o_ref.dtype)


def _rwkv7(proj, mu, w0, w_up, a0, a_up, g_up, k_k, k_a, r_k, ln_w, ln_b, TS=256):
    B, S, _ = proj.shape
    W = RWKV_W
    row = lambda n: pl.BlockSpec((1, n), lambda b, s: (0, 0))
    full = lambda a: pl.BlockSpec(a.shape, lambda b, s: (0, 0))
    buf = lambda: pltpu.VMEM((TS, W), F32)
    return pl.pallas_call(
        functools.partial(_rwkv_kernel, TS=TS),
        out_shape=jax.ShapeDtypeStruct((B, S, W), BF16),
        grid=(B, S // TS),
        in_specs=[
            pl.BlockSpec((1, TS, RWKV_IN), lambda b, s: (b, s, COL_RWKV // RWKV_IN)),
            row(RWKV_IN), row(W), full(w_up), row(W), full(a_up), full(g_up),
            row(W), row(W), row(W), row(W), row(W),
        ],
        out_specs=pl.BlockSpec((1, TS, W), lambda b, s: (b, s, 0)),
        scratch_shapes=[
            pltpu.VMEM((RWKV_HEADS, HEAD_DIM, HEAD_DIM), F32),
            pltpu.VMEM((1, RWKV_IN), F32),
            buf(), buf(), buf(), buf(), buf(), buf(), buf(),
        ],
        compiler_params=pltpu.CompilerParams(
            dimension_semantics=("parallel", "arbitrary"), vmem_limit_bytes=VMEM_LIMIT),
        name="rwkv7",
    )(proj, mu.reshape(1, -1), w0.reshape(1, W), w_up, a0.reshape(1, W), a_up, g_up,
      k_k.reshape(1, W), k_a.reshape(1, W), r_k.reshape(1, W), ln_w.reshape(1, W), ln_b.reshape(1, W))


def _post_kernel(h_ref, oret_ref, osb_ref, orw_ref, p_ref, wo_ref, gm_ref, w1_ref, w2_ref,
                 gp_ref, wpg_ref, wpe_ref, out_ref, *, ff_chunk):
    h = h_ref[...]
    h = h + jnp.dot(oret_ref[...], wo_ref[0:RET_W, :], preferred_element_type=F32)
    h = h + jnp.dot(osb_ref[...], wo_ref[RET_W:RET_W + SB_W, :], preferred_element_type=F32)
    h = h + jnp.dot(orw_ref[...], wo_ref[RET_W + SB_W:, :], preferred_element_type=F32)

    n = _rms(h, gm_ref[...]).astype(BF16)
    out_ref[...] = h
    for c in range(D_FF // ff_chunk):
        cols = slice(c * ff_chunk, (c + 1) * ff_chunk)
        u = jnp.dot(n, w1_ref[:, cols], preferred_element_type=F32)
        u = jnp.square(jnp.maximum(u, 0.0)).astype(BF16)
        out_ref[...] += jnp.dot(u, w2_ref[cols, :], preferred_element_type=F32)
    h = out_ref[...]

    n = _rms(h, gp_ref[...]).astype(BF16)
    gate = _sigmoid(jnp.dot(n, wpg_ref[...], preferred_element_type=F32))
    pe = jnp.dot(p_ref[...].astype(BF16), wpe_ref[...], preferred_element_type=F32)
    out_ref[...] = h + gate * pe


def _post(h, o_ret, o_sb, o_rw, p, w_o, g_mlp, w1, w2, g_ple, w_pg, w_pe, tm=512, ff_chunk=1024):
    M, D = h.shape
    rows = lambda n: pl.BlockSpec((tm, n), lambda i: (i, 0))
    const = lambda a: pl.BlockSpec(a.shape, lambda i: (0, 0), pipeline_mode=pl.Buffered(1))
    g_mlp = g_mlp.reshape(1, D)
    g_ple = g_ple.reshape(1, D)
    return pl.pallas_call(
        functools.partial(_post_kernel, ff_chunk=ff_chunk),
        out_shape=jax.ShapeDtypeStruct((M, D), F32),
        grid=(M // tm,),
        in_specs=[
            rows(D), rows(RET_W), rows(SB_W), rows(RWKV_W), rows(PLE_DIM),
            const(w_o), const(g_mlp), const(w1), const(w2), const(g_ple), const(w_pg), const(w_pe),
        ],
        out_specs=rows(D),
        compiler_params=pltpu.CompilerParams(
            dimension_semantics=("parallel",), vmem_limit_bytes=VMEM_LIMIT),
        name="post",
    )(h, o_ret, o_sb, o_rw, p, w_o, g_mlp, w1, w2, g_ple, w_pg, w_pe)


def _final_norm_kernel(h_ref, g_ref, o_ref):
    o_ref[...] = _rms(h_ref[...], g_ref[...])


def _final_norm(h, g, tm=1024):
    M, D = h.shape
    return pl.pallas_call(
        _final_norm_kernel,
        out_shape=jax.ShapeDtypeStruct((M, D), F32),
        grid=(M // tm,),
        in_specs=[pl.BlockSpec((tm, D), lambda i: (i, 0)), pl.BlockSpec((1, D), lambda i: (0, 0))],
        out_specs=pl.BlockSpec((tm, D), lambda i: (i, 0)),
        compiler_params=pltpu.CompilerParams(dimension_semantics=("parallel",)),
        name="final_norm",
    )(h, g.reshape(1, D))


def _proj_column_order():
    sbk0 = RET_IN + SB_W
    sbv0 = RET_IN + 2 * SB_W
    cols = [np.arange(0, RET_IN), np.arange(RET_IN + SB_IN, IN_W), np.arange(RET_IN, RET_IN + SB_W)]
    for h in range(SB_HEADS):
        cols.append(np.arange(sbk0 + h * HEAD_DIM, sbk0 + (h + 1) * HEAD_DIM))
        cols.append(np.arange(sbv0 + h * HEAD_DIM, sbv0 + (h + 1) * HEAD_DIM))
    return np.concatenate(cols)


def _rope_tables(S):
    pos = jnp.arange(S, dtype=F32)
    inv_freq = 1.0 / (ROPE_BASE ** jnp.linspace(0.0, 1.0, HEAD_DIM // 2, dtype=F32))
    ang = pos[:, None] * inv_freq[None, :]
    cos, sin = jnp.cos(ang), jnp.sin(ang)
    return jnp.concatenate([cos, cos, cos, cos], axis=-1), jnp.concatenate([-sin, sin, -sin, sin], axis=-1)


def kernel(x, p, norm_mix_g, norm_mlp_g, norm_ple_g, w_in, ret_norm_g, rwkv_mu, rwkv_w0, rwkv_w_up, rwkv_a0, rwkv_a_up, rwkv_g_up, rwkv_k_k, rwkv_k_a, rwkv_r_k, rwkv_ln_w, rwkv_ln_b, w_o, w_mlp_in, w_mlp_out, w_pe, w_pg, final_norm_g):
    B, S, D = x.shape
    depth = w_in.shape[0]
    M = B * S
    w_in_p = jnp.take(w_in, jnp.asarray(_proj_column_order()), axis=2).astype(BF16)
    w_o_b = w_o.astype(BF16)
    w1_b = w_mlp_in.astype(BF16)
    w2_b = w_mlp_out.astype(BF16)
    w_pg_b = w_pg.astype(BF16)
    w_pe_b = w_pe.astype(BF16)
    cos4, sin4 = _rope_tables(S)

    h = x.reshape(M, D)
    for i in range(depth):
        proj = _norm_proj(h, norm_mix_g[i], w_in_p[i]).reshape(B, S, IN_W)
        o_ret = _retention(proj, cos4, sin4, ret_norm_g[i])
        o_sb = _stick_breaking(proj)
        o_rw = _rwkv7(proj, rwkv_mu[i], rwkv_w0[i], rwkv_w_up[i], rwkv_a0[i], rwkv_a_up[i], rwkv_g_up[i],
                      rwkv_k_k[i], rwkv_k_a[i], rwkv_r_k[i], rwkv_ln_w[i], rwkv_ln_b[i])
        h = _post(h, o_ret.reshape(M, RET_W), o_sb.reshape(M, SB_W), o_rw.reshape(M, RWKV_W),
                  p[i].reshape(M, PLE_DIM), w_o_b[i], norm_mlp_g[i], w1_b[i], w2_b[i],
                  norm_ple_g[i], w_pg_b[i], w_pe_b[i])
    return _final_norm(h, final_norm_g).reshape(B, S, D)
```

```python
import functools
import math

import numpy as np
import jax
import jax.numpy as jnp
from jax import lax
from jax.experimental import pallas as pl
from jax.experimental.pallas import tpu as pltpu

F32 = jnp.float32
BF16 = jnp.bfloat16

D_MODEL = 1024
HEAD_DIM = 64
RET_W = 512
SB_W = 256
RWKV_W = 256
RET_HEADS = 8
SB_HEADS = 4
RWKV_HEADS = 4
DECAY_LORA = 64
AAA_LORA = 64
GATE_LORA = 128
RWKV_IN = 3 * RWKV_W + DECAY_LORA + AAA_LORA + GATE_LORA
RET_IN = 4 * RET_W
SB_IN = 3 * SB_W
IN_W = RET_IN + SB_IN + RWKV_IN
D_FF = 4 * D_MODEL
PLE_DIM = 256
RET_CHUNK = 64
SB_BLOCK = 128
ROPE_BASE = 10000.0
NORM_EPS = 1e-6
RWKV_GN_EPS = 64e-5

COL_RET = 0
COL_RWKV = RET_IN
MAIN_W = RET_IN + RWKV_IN

RET_LOG_G = tuple(math.log(1.0 - 2.0 ** (-5.0 - h)) for h in range(RET_HEADS))
SB_SKIP_LOG = -104.0
RWKV_CHUNK = 64

VMEM_LIMIT = 56 * 1024 * 1024

NT = (((1,), (1,)), ((), ()))
TN = (((0,), (0,)), ((), ()))


def _mm(a, b, dims=None, exact=False):
    if dims is None:
        dims = (((a.ndim - 1,), (0,)), ((), ()))
    if exact:
        return lax.dot_general(a, b, dims, precision=lax.Precision.HIGHEST, preferred_element_type=F32)
    return lax.dot_general(a.astype(BF16), b.astype(BF16), dims, preferred_element_type=F32)


def _rms(x, g):
    return x * lax.rsqrt(jnp.mean(x * x, axis=-1, keepdims=True) + NORM_EPS) * g


def _softplus(z):
    return jnp.maximum(z, 0.0) + jnp.log(1.0 + jnp.exp(-jnp.abs(z)))


def _sigmoid(z):
    return 1.0 / (1.0 + jnp.exp(-z))


def _norm_proj_kernel(h_ref, g_ref, w_ref, o_ref, sb_ref):
    n = _rms(h_ref[...], g_ref[...]).astype(BF16)
    o_ref[...] = jnp.dot(n, w_ref[:, 0:MAIN_W], preferred_element_type=F32)
    sb_ref[...] = jnp.dot(n, w_ref[:, MAIN_W:], preferred_element_type=F32).astype(BF16)


def _norm_proj(h, g, w, tm=512):
    M, D = h.shape
    N = w.shape[1]
    return pl.pallas_call(
        _norm_proj_kernel,
        out_shape=(jax.ShapeDtypeStruct((M, MAIN_W), F32), jax.ShapeDtypeStruct((M, SB_IN), BF16)),
        grid=(M // tm,),
        in_specs=[
            pl.BlockSpec((tm, D), lambda i: (i, 0)),
            pl.BlockSpec((1, D), lambda i: (0, 0)),
            pl.BlockSpec((D, N), lambda i: (0, 0)),
        ],
        out_specs=(pl.BlockSpec((tm, MAIN_W), lambda i: (i, 0)), pl.BlockSpec((tm, SB_IN), lambda i: (i, 0))),
        compiler_params=pltpu.CompilerParams(
            dimension_semantics=("parallel",), vmem_limit_bytes=VMEM_LIMIT),
        name="norm_proj",
    )(h, g.reshape(1, D), w)


def _retention_kernel(q_ref, k_ref, v_ref, g_ref, cos_ref, sin_ref, ng_ref, o_ref, state_ref, *, T):
    @pl.when(pl.program_id(1) == 0)
    def _():
        state_ref[...] = jnp.zeros_like(state_ref)

    cos = jnp.concatenate([cos_ref[...]] * 4, axis=-1)
    sin = jnp.concatenate([sin_ref[...]] * 4, axis=-1)
    lane = lax.broadcasted_iota(jnp.int32, (T, RET_W), 1)
    first_half = (lane % HEAD_DIM) < (HEAD_DIM // 2)

    def rope(t):
        swapped = jnp.where(first_half, pltpu.roll(t, RET_W - HEAD_DIM // 2, 1), pltpu.roll(t, HEAD_DIM // 2, 1))
        return t * cos + swapped * sin

    q = rope(q_ref[0])
    k = rope(k_ref[0]) * (HEAD_DIM ** -0.5)
    v = v_ref[0]
    gate = g_ref[0]
    gate = gate * _sigmoid(gate)
    ng = ng_ref[...]

    n_idx = lax.broadcasted_iota(jnp.int32, (T, T), 0)
    m_idx = lax.broadcasted_iota(jnp.int32, (T, T), 1)
    visible = (m_idx <= n_idx) | ((m_idx // RET_CHUNK) == (n_idx // RET_CHUNK))
    dist = jnp.abs(n_idx - m_idx).astype(F32)
    pos = lax.broadcasted_iota(jnp.int32, (T, 1), 0).astype(F32)

    for h in range(RET_HEADS):
        lg = RET_LOG_G[h]
        sl = slice(h * HEAD_DIM, (h + 1) * HEAD_DIM)
        qh, kh, vh = q[:, sl], k[:, sl], v[:, sl]
        decay = jnp.where(visible, jnp.exp(dist * lg), 0.0)
        scores = _mm(qh, kh, NT) * decay
        st = state_ref[h]
        o = _mm(scores, vh) + _mm(qh * jnp.exp((pos + 1.0) * lg), st)
        state_ref[h] = st * math.exp(T * lg) + _mm(kh * jnp.exp((T - 1.0 - pos) * lg), vh, TN)
        o = o * lax.rsqrt(jnp.mean(o * o, axis=-1, keepdims=True) + NORM_EPS)
        o_ref[0, :, sl] = (o * ng[:, sl] * gate[:, sl]).astype(o_ref.dtype)


def _retention(proj, cos4, sin4, norm_g, T=256):
    B, S, _ = proj.shape
    blk = lambda c: pl.BlockSpec((1, T, RET_W), lambda b, s, c=c: (b, s, c))
    return pl.pallas_call(
        functools.partial(_retention_kernel, T=T),
        out_shape=jax.ShapeDtypeStruct((B, S, RET_W), BF16),
        grid=(B, S // T),
        in_specs=[
            blk(0), blk(1), blk(2), blk(3),
            pl.BlockSpec((T, 128), lambda b, s: (s, 0)),
            pl.BlockSpec((T, 128), lambda b, s: (s, 0)),
            pl.BlockSpec((1, RET_W), lambda b, s: (0, 0)),
        ],
        out_specs=pl.BlockSpec((1, T, RET_W), lambda b, s: (b, s, 0)),
        scratch_shapes=[pltpu.VMEM((RET_HEADS, HEAD_DIM, HEAD_DIM), F32)],
        compiler_params=pltpu.CompilerParams(
            dimension_semantics=("parallel", "arbitrary"), vmem_limit_bytes=VMEM_LIMIT),
        name="retention",
    )(proj, proj, proj, proj, cos4, sin4, norm_g.reshape(1, RET_W))


def _stick_kernel(q_ref, kv_ref, o_ref, *, TB):
    i = pl.program_id(1)
    row = lax.broadcasted_iota(jnp.int32, (TB, TB), 0)
    col = lax.broadcasted_iota(jnp.int32, (TB, TB), 1)
    strict = col < row
    rev_incl = (row >= col).astype(BF16)
    low_lanes = lax.broadcasted_iota(jnp.int32, (TB, 2 * HEAD_DIM), 1) < HEAD_DIM

    qm = []
    for h in range(SB_HEADS):
        qp = q_ref[0, :, (h // 2) * 128:(h // 2 + 1) * 128] * (HEAD_DIM ** -0.5)
        qm.append(jnp.where(low_lanes if h % 2 == 0 else ~low_lanes, qp, jnp.zeros_like(qp)))

    def block(n, o, acc, diagonal):
        rows = pl.ds(pl.multiple_of(n * TB, TB), TB)
        heads = range(SB_HEADS)
        kp = [kv_ref[0, rows, j * 128:(j + 1) * 128] for j in range(SB_HEADS // 2)]
        vp = [kv_ref[0, rows, SB_W + j * 128:SB_W + (j + 1) * 128] for j in range(SB_HEADS // 2)]
        z = [lax.dot_general(qm[h], kp[h // 2], NT, preferred_element_type=F32) for h in heads]
        ls = [-_softplus(z[h]) for h in heads]
        if diagonal:
            ls = [jnp.where(strict, ls[h], 0.0) for h in heads]
        cum = [jnp.dot(ls[h].astype(BF16), rev_incl, preferred_element_type=F32) for h in heads]
        if acc is None:
            w = [jnp.exp(z[h] + cum[h]) for h in heads]
        else:
            w = [jnp.exp(z[h] + cum[h] + acc[h]) for h in heads]
        if diagonal:
            w = [jnp.where(strict, w[h], 0.0) for h in heads]
        pv = [jnp.dot(w[h].astype(BF16), vp[h // 2], preferred_element_type=F32) for h in heads]
        tot = [jnp.sum(ls[h], axis=-1, keepdims=True) for h in heads]
        new_acc = tot if acc is None else [acc[h] + tot[h] for h in heads]
        upd = [jnp.where(low_lanes, pv[2 * j], pv[2 * j + 1]) for j in range(SB_HEADS // 2)]
        new_o = upd if o is None else [o[j] + upd[j] for j in range(SB_HEADS // 2)]
        return tuple(new_o), tuple(new_acc)

    def top_of(acc):
        return functools.reduce(jnp.maximum, [jnp.max(a) for a in acc])

    o, acc = block(i, None, None, True)

    def cond(c):
        n, _, _, top = c
        return jnp.logical_and(n >= 0, top > SB_SKIP_LOG)

    def body(c):
        n, o, acc, _ = c
        o, acc = block(n, o, acc, False)
        return n - 1, o, acc, top_of(acc)

    _, o, _, _ = lax.while_loop(cond, body, (i - 1, o, acc, top_of(acc)))
    for j in range(SB_HEADS // 2):
        o_ref[0, :, j * 128:(j + 1) * 128] = o[j].astype(o_ref.dtype)


def _stick_breaking(proj_sb, TB=SB_BLOCK):
    B, S, _ = proj_sb.shape
    return pl.pallas_call(
        functools.partial(_stick_kernel, TB=TB),
        out_shape=jax.ShapeDtypeStruct((B, S, SB_W), BF16),
        grid=(B, S // TB),
        in_specs=[
            pl.BlockSpec((1, TB, SB_W), lambda b, i: (b, i, 2)),
            pl.BlockSpec((1, S, 2 * SB_W), lambda b, i: (b, 0, 0)),
        ],
        out_specs=pl.BlockSpec((1, TB, SB_W), lambda b, i: (b, i, 0)),
        compiler_params=pltpu.CompilerParams(
            dimension_semantics=("parallel", "arbitrary"), vmem_limit_bytes=VMEM_LIMIT),
        name="stick_breaking",
    )(proj_sb, proj_sb)


def _rwkv_kernel(z_ref, mu_ref, w0_ref, wup_ref, a0_ref, aup_ref, gup_ref, kk_ref, ka_ref, rk_ref,
                 lnw_ref, lnb_ref, o_ref,
                 state_ref, prev_ref, sola_s, solv_s, mrb_s, yfix_s, sfix_s, y_s, *, TS):
    C = RWKV_CHUNK
    W = RWKV_W
    NC = TS // C

    @pl.when(pl.program_id(1) == 0)
    def _():
        state_ref[...] = jnp.zeros_like(state_ref)
        prev_ref[...] = jnp.zeros_like(prev_ref)

    z = z_ref[0]
    first_row = lax.broadcasted_iota(jnp.int32, (TS, 1), 0) == 0
    z_prev = jnp.where(first_row, prev_ref[...], pltpu.roll(z, 1, 0))
    prev_ref[...] = z[TS - 1:TS, :]
    z = z + mu_ref[...] * (z_prev - z)

    r, k, v = z[:, 0:W], z[:, W:2 * W], z[:, 2 * W:3 * W]
    wd = z[:, 3 * W:3 * W + DECAY_LORA]
    ad = z[:, 3 * W + DECAY_LORA:3 * W + DECAY_LORA + AAA_LORA]
    gd = z[:, 3 * W + DECAY_LORA + AAA_LORA:]

    gi = lax.broadcasted_iota(jnp.int32, (W, W), 0) // HEAD_DIM
    gj = lax.broadcasted_iota(jnp.int32, (W, W), 1) // HEAD_DIM
    same_head = (gi == gj).astype(BF16)

    def head_sum(x):
        return jnp.dot(x.astype(BF16), same_head, preferred_element_type=F32)

    log_w = -_softplus(-(w0_ref[...] + _mm(jnp.tanh(wd), wup_ref[...]))) - 0.5
    lw = -jnp.exp(log_w)
    a = _sigmoid(a0_ref[...] + _mm(ad, aup_ref[...]))
    g = _mm(_sigmoid(gd), gup_ref[...])
    kk = k * kk_ref[...]
    kk = kk / jnp.maximum(jnp.sqrt(head_sum(kk * kk)), 1e-12)
    k_mod = k * (1.0 + (a - 1.0) * ka_ref[...])
    a_vec = -kk
    b_vec = kk * a

    ti = lax.broadcasted_iota(jnp.int32, (C, C), 0)
    tj = lax.broadcasted_iota(jnp.int32, (C, C), 1)
    strict = tj < ti
    incl = tj <= ti
    tri_incl = incl.astype(F32)

    systems = [(c, h) for c in range(NC) for h in range(RWKV_HEADS)]
    head = lambda t, h: t[:, h * HEAD_DIM:(h + 1) * HEAD_DIM]
    dot = functools.partial(jnp.dot, preferred_element_type=F32)
    dot_nt = functools.partial(lax.dot_general, dimension_numbers=NT, preferred_element_type=F32)
    dot_tn = functools.partial(lax.dot_general, dimension_numbers=TN, preferred_element_type=F32)

    at_f, at_b, rt_b, bt_b, kt_b, bh_b, kh_b, v_b, p_end = [], [], [], [], [], [], [], [], []
    for c in range(NC):
        rows = slice(c * C, (c + 1) * C)
        lw_c = lw[rows]
        cs = _mm(tri_incl, lw_c, exact=True)
        e_incl = jnp.exp(cs)
        e_inv = jnp.exp(-cs)
        p_c = e_incl[C - 1:C, :]
        at = a_vec[rows] * jnp.exp(cs - lw_c)
        bt = b_vec[rows] * e_inv
        kt = k_mod[rows] * e_inv
        at_f.append(at)
        at_b.append(at.astype(BF16))
        rt_b.append((r[rows] * e_incl).astype(BF16))
        bt_b.append(bt.astype(BF16))
        kt_b.append(kt.astype(BF16))
        bh_b.append((bt * p_c).astype(BF16))
        kh_b.append((kt * p_c).astype(BF16))
        v_b.append(v[rows].astype(BF16))
        p_end.append(p_c)

    vh = [head(v_b[c], h) for c, h in systems]
    gram = [dot_nt(jnp.concatenate([head(at_b[c], h), head(rt_b[c], h)], axis=0),
                   jnp.concatenate([head(bt_b[c], h), head(kt_b[c], h)], axis=0)) for c, h in systems]
    a_ab = [jnp.where(strict, gm[0:C, 0:C], 0.0).astype(BF16) for gm in gram]
    a_ak = [jnp.where(strict, gm[0:C, C:2 * C], 0.0).astype(BF16) for gm in gram]
    m_rk = [jnp.where(incl, gm[C:2 * C, C:2 * C], 0.0).astype(BF16) for gm in gram]
    for i, gm in enumerate(gram):
        mrb_s[i] = jnp.where(incl, gm[C:2 * C, 0:C], 0.0)
    akv = [dot(a_ak[i], vh[i]) for i in range(len(systems))]
    sol = [jnp.concatenate([head(at_f[c], h), akv[i]], axis=1) for i, (c, h) in enumerate(systems)]
    pw = a_ab
    for level in range(int(math.log2(C))):
        if level > 0:
            pw = [dot(m, m).astype(BF16) for m in pw]
        sol = [x + dot(m, x.astype(BF16)) for m, x in zip(pw, sol)]
    for i, (c, h) in enumerate(systems):
        sola_s[i] = sol[i][:, 0:HEAD_DIM]
        solv_s[i] = sol[i][:, HEAD_DIM:]
        yfix_s[i] = dot(m_rk[i], vh[i])
        sfix_s[i] = dot_tn(vh[i], head(kh_b[c], h))

    heads = range(RWKV_HEADS)
    for c in range(NC):
        rows = slice(c * C, (c + 1) * C)
        base = c * RWKV_HEADS
        st = [state_ref[h] for h in heads]
        st_b = [s.astype(BF16) for s in st]
        u = [dot_nt(sola_s[base + h].astype(BF16), st_b[h]) + solv_s[base + h] for h in heads]
        u_b = [x.astype(BF16) for x in u]
        ys = [dot_nt(head(rt_b[c], h), st_b[h]) for h in heads]
        yu = [dot(mrb_s[base + h].astype(BF16), u_b[h]) for h in heads]
        su = [dot_tn(u_b[h], head(bh_b[c], h)) for h in heads]
        for h in heads:
            y_s[rows, h * HEAD_DIM:(h + 1) * HEAD_DIM] = ys[h] + yu[h] + yfix_s[base + h]
            state_ref[h] = st[h] * head(p_end[c], h) + su[h] + sfix_s[base + h]

    y = y_s[...]
    mean = head_sum(y) * (1.0 / HEAD_DIM)
    d = y - mean
    var = head_sum(d * d) * (1.0 / HEAD_DIM)
    y = d * lax.rsqrt(var + RWKV_GN_EPS) * lnw_ref[...] + lnb_ref[...]
    bonus = head_sum(r * k_mod * rk_ref[...]) * v
    o_ref[0] = ((y + bonus) * g).astype(o_ref.dtype)


def _rwkv7(proj, mu, w0, w_up, a0, a_up, g_up, k_k, k_a, r_k, ln_w, ln_b, TS=256):
    B, S, _ = proj.shape
    W = RWKV_W
    n_sys = (TS // RWKV_CHUNK) * RWKV_HEADS
    row = lambda n: pl.BlockSpec((1, n), lambda b, s: (0, 0))
    full = lambda a: pl.BlockSpec(a.shape, lambda b, s: (0, 0))
    per_sys = lambda: pltpu.VMEM((n_sys, RWKV_CHUNK, HEAD_DIM), F32)
    return pl.pallas_call(
        functools.partial(_rwkv_kernel, TS=TS),
        out_shape=jax.ShapeDtypeStruct((B, S, W), BF16),
        grid=(B, S // TS),
        in_specs=[
            pl.BlockSpec((1, TS, RWKV_IN), lambda b, s: (b, s, COL_RWKV // RWKV_IN)),
            row(RWKV_IN), row(W), full(w_up), row(W), full(a_up), full(g_up),
            row(W), row(W), row(W), row(W), row(W),
        ],
        out_specs=pl.BlockSpec((1, TS, W), lambda b, s: (b, s, 0)),
        scratch_shapes=[
            pltpu.VMEM((RWKV_HEADS, HEAD_DIM, HEAD_DIM), F32),
            pltpu.VMEM((1, RWKV_IN), F32),
            per_sys(), per_sys(), per_sys(), per_sys(), per_sys(),
            pltpu.VMEM((TS, W), F32),
        ],
        compiler_params=pltpu.CompilerParams(
            dimension_semantics=("parallel", "arbitrary"), vmem_limit_bytes=VMEM_LIMIT),
        name="rwkv7",
    )(proj, mu.reshape(1, -1), w0.reshape(1, W), w_up, a0.reshape(1, W), a_up, g_up,
      k_k.reshape(1, W), k_a.reshape(1, W), r_k.reshape(1, W), ln_w.reshape(1, W), ln_b.reshape(1, W))


def _post_kernel(h_ref, oret_ref, osb_ref, orw_ref, p_ref, wo_ref, gm_ref, w1_ref, w2_ref,
                 gp_ref, wpg_ref, wpe_ref, out_ref, *, ff_chunk):
    h = h_ref[...]
    h = h + jnp.dot(oret_ref[...], wo_ref[0:RET_W, :], preferred_element_type=F32)
    h = h + jnp.dot(osb_ref[...], wo_ref[RET_W:RET_W + SB_W, :], preferred_element_type=F32)
    h = h + jnp.dot(orw_ref[...], wo_ref[RET_W + SB_W:, :], preferred_element_type=F32)

    n = _rms(h, gm_ref[...]).astype(BF16)
    out_ref[...] = h
    for c in range(D_FF // ff_chunk):
        cols = slice(c * ff_chunk, (c + 1) * ff_chunk)
        u = jnp.dot(n, w1_ref[:, cols], preferred_element_type=F32)
        u = jnp.square(jnp.maximum(u, 0.0)).astype(BF16)
        out_ref[...] += jnp.dot(u, w2_ref[cols, :], preferred_element_type=F32)
    h = out_ref[...]

    n = _rms(h, gp_ref[...]).astype(BF16)
    gate = _sigmoid(jnp.dot(n, wpg_ref[...], preferred_element_type=F32))
    pe = jnp.dot(p_ref[...].astype(BF16), wpe_ref[...], preferred_element_type=F32)
    out_ref[...] = h + gate * pe


def _post(h, o_ret, o_sb, o_rw, p, w_o, g_mlp, w1, w2, g_ple, w_pg, w_pe, tm=512, ff_chunk=1024):
    M, D = h.shape
    rows = lambda n: pl.BlockSpec((tm, n), lambda i: (i, 0))
    const = lambda a: pl.BlockSpec(a.shape, lambda i: (0, 0), pipeline_mode=pl.Buffered(1))
    g_mlp = g_mlp.reshape(1, D)
    g_ple = g_ple.reshape(1, D)
    return pl.pallas_call(
        functools.partial(_post_kernel, ff_chunk=ff_chunk),
        out_shape=jax.ShapeDtypeStruct((M, D), F32),
        grid=(M // tm,),
        in_specs=[
            rows(D), rows(RET_W), rows(SB_W), rows(RWKV_W), rows(PLE_DIM),
            const(w_o), const(g_mlp), const(w1), const(w2), const(g_ple), const(w_pg), const(w_pe),
        ],
        out_specs=rows(D),
        compiler_params=pltpu.CompilerParams(
            dimension_semantics=("parallel",), vmem_limit_bytes=VMEM_LIMIT),
        name="post",
    )(h, o_ret, o_sb, o_rw, p, w_o, g_mlp, w1, w2, g_ple, w_pg, w_pe)


def _final_norm_kernel(h_ref, g_ref, o_ref):
    o_ref[...] = _rms(h_ref[...], g_ref[...])


def _final_norm(h, g, tm=1024):
    M, D = h.shape
    return pl.pallas_call(
        _final_norm_kernel,
        out_shape=jax.ShapeDtypeStruct((M, D), F32),
        grid=(M // tm,),
        in_specs=[pl.BlockSpec((tm, D), lambda i: (i, 0)), pl.BlockSpec((1, D), lambda i: (0, 0))],
        out_specs=pl.BlockSpec((tm, D), lambda i: (i, 0)),
        compiler_params=pltpu.CompilerParams(dimension_semantics=("parallel",)),
        name="final_norm",
    )(h, g.reshape(1, D))


def _proj_column_order():
    sbq = np.arange(RET_IN, RET_IN + SB_W)
    sbk = np.arange(RET_IN + SB_W, RET_IN + 2 * SB_W)
    sbv = np.arange(RET_IN + 2 * SB_W, RET_IN + 3 * SB_W)
    return np.concatenate([np.arange(0, RET_IN), np.arange(RET_IN + SB_IN, IN_W), sbk, sbv, sbq])


def _rope_tables(S):
    pos = jnp.arange(S, dtype=F32)
    inv_freq = 1.0 / (ROPE_BASE ** jnp.linspace(0.0, 1.0, HEAD_DIM // 2, dtype=F32))
    ang = pos[:, None] * inv_freq[None, :]
    cos, sin = jnp.cos(ang), jnp.sin(ang)
    return jnp.concatenate([cos, cos, cos, cos], axis=-1), jnp.concatenate([-sin, sin, -sin, sin], axis=-1)


def kernel(x, p, norm_mix_g, norm_mlp_g, norm_ple_g, w_in, ret_norm_g, rwkv_mu, rwkv_w0, rwkv_w_up, rwkv_a0, rwkv_a_up, rwkv_g_up, rwkv_k_k, rwkv_k_a, rwkv_r_k, rwkv_ln_w, rwkv_ln_b, w_o, w_mlp_in, w_mlp_out, w_pe, w_pg, final_norm_g):
    B, S, D = x.shape
    depth = w_in.shape[0]
    M = B * S
    w_in_p = jnp.take(w_in, jnp.asarray(_proj_column_order()), axis=2).astype(BF16)
    w_o_b = w_o.astype(BF16)
    w1_b = w_mlp_in.astype(BF16)
    w2_b = w_mlp_out.astype(BF16)
    w_pg_b = w_pg.astype(BF16)
    w_pe_b = w_pe.astype(BF16)
    cos4, sin4 = _rope_tables(S)

    h = x.reshape(M, D)
    for i in range(depth):
        proj, proj_sb = _norm_proj(h, norm_mix_g[i], w_in_p[i])
        proj = proj.reshape(B, S, MAIN_W)
        o_ret = _retention(proj, cos4, sin4, ret_norm_g[i])
        o_sb = _stick_breaking(proj_sb.reshape(B, S, SB_IN))
        o_rw = _rwkv7(proj, rwkv_mu[i], rwkv_w0[i], rwkv_w_up[i], rwkv_a0[i], rwkv_a_up[i], rwkv_g_up[i],
                      rwkv_k_k[i], rwkv_k_a[i], rwkv_r_k[i], rwkv_ln_w[i], rwkv_ln_b[i])
        h = _post(h, o_ret.reshape(M, RET_W), o_sb.reshape(M, SB_W), o_rw.reshape(M, RWKV_W),
                  p[i].reshape(M, PLE_DIM), w_o_b[i], norm_mlp_g[i], w1_b[i], w2_b[i],
                  norm_ple_g[i], w_pg_b[i], w_pe_b[i])
    return _final_norm(h, final_norm_g).reshape(B, S, D)
```

```python
import functools
import math

import numpy as np
import jax
import jax.numpy as jnp
from jax import lax
from jax.experimental import pallas as pl
from jax.experimental.pallas import tpu as pltpu

F32 = jnp.float32
BF16 = jnp.bfloat16

D_MODEL = 1024
HEAD_DIM = 64
RET_W = 512
SB_W = 256
RWKV_W = 256
RET_HEADS = 8
SB_HEADS = 4
RWKV_HEADS = 4
DECAY_LORA = 64
AAA_LORA = 64
GATE_LORA = 128
RWKV_IN = 3 * RWKV_W + DECAY_LORA + AAA_LORA + GATE_LORA
RET_IN = 4 * RET_W
SB_IN = 3 * SB_W
IN_W = RET_IN + SB_IN + RWKV_IN
D_FF = 4 * D_MODEL
PLE_DIM = 256
RET_CHUNK = 64
SB_BLOCK = 256
ROPE_BASE = 10000.0
NORM_EPS = 1e-6
RWKV_GN_EPS = 64e-5

COL_RET = 0
COL_RWKV = RET_IN
MAIN_W = RET_IN + RWKV_IN

RET_LOG_G = tuple(math.log(1.0 - 2.0 ** (-5.0 - h)) for h in range(RET_HEADS))
SB_SKIP_LOG = -104.0
RWKV_CHUNK = 64

VMEM_LIMIT = 56 * 1024 * 1024

NT = (((1,), (1,)), ((), ()))
TN = (((0,), (0,)), ((), ()))


def _mm(a, b, dims=None, exact=False):
    if dims is None:
        dims = (((a.ndim - 1,), (0,)), ((), ()))
    if exact:
        return lax.dot_general(a, b, dims, precision=lax.Precision.HIGHEST, preferred_element_type=F32)
    return lax.dot_general(a.astype(BF16), b.astype(BF16), dims, preferred_element_type=F32)


def _rms(x, g):
    return x * lax.rsqrt(jnp.mean(x * x, axis=-1, keepdims=True) + NORM_EPS) * g


def _softplus(z):
    return jnp.maximum(z, 0.0) + jnp.log(1.0 + jnp.exp(-jnp.abs(z)))


def _sigmoid(z):
    return 1.0 / (1.0 + jnp.exp(-z))


def _norm_proj_kernel(h_ref, g_ref, w_ref, o_ref, sb_ref):
    n = _rms(h_ref[...], g_ref[...]).astype(BF16)
    o_ref[...] = jnp.dot(n, w_ref[:, 0:MAIN_W], preferred_element_type=F32)
    sb_ref[...] = jnp.dot(n, w_ref[:, MAIN_W:], preferred_element_type=F32).astype(BF16)


def _norm_proj(h, g, w, layer, tm=512):
    M, D = h.shape
    N = w.shape[2]
    return pl.pallas_call(
        _norm_proj_kernel,
        out_shape=(jax.ShapeDtypeStruct((M, MAIN_W), F32), jax.ShapeDtypeStruct((M, SB_IN), BF16)),
        grid=(M // tm,),
        in_specs=[
            pl.BlockSpec((tm, D), lambda i: (i, 0)),
            pl.BlockSpec((None, 1, D), lambda i: (layer, 0, 0)),
            pl.BlockSpec((None, D, N), lambda i: (layer, 0, 0)),
        ],
        out_specs=(pl.BlockSpec((tm, MAIN_W), lambda i: (i, 0)), pl.BlockSpec((tm, SB_IN), lambda i: (i, 0))),
        compiler_params=pltpu.CompilerParams(
            dimension_semantics=("parallel",), vmem_limit_bytes=VMEM_LIMIT),
        name="norm_proj",
    )(h, g, w)


def _retention_kernel(q_ref, k_ref, v_ref, g_ref, cos_ref, sin_ref, ng_ref, dec_ref, qdec_ref, kdec_ref,
                      sdec_ref, own_ref, hsum_ref, o_ref, state_ref, *, T):
    @pl.when(pl.program_id(1) == 0)
    def _():
        state_ref[...] = jnp.zeros_like(state_ref)

    cos = cos_ref[...]
    sin = sin_ref[...]
    lane = lax.broadcasted_iota(jnp.int32, (T, 128), 1)
    k_even = (lane % HEAD_DIM) < (HEAD_DIM // 2)
    v_even = lane < HEAD_DIM
    own = own_ref[...]
    hsum = hsum_ref[...]
    dot = functools.partial(jnp.dot, preferred_element_type=F32)

    for j in range(RET_HEADS // 2):
        cols = slice(j * 128, (j + 1) * 128)
        q = q_ref[0, :, cols]
        k = k_ref[0, :, cols]
        q = q * cos + pltpu.roll(q, HEAD_DIM, 1) * sin
        k = (k * cos + pltpu.roll(k, HEAD_DIM, 1) * sin) * (HEAD_DIM ** -0.5)
        v_b = v_ref[0, :, cols].astype(BF16)
        k_b = k.astype(BF16)
        q_b = q.astype(BF16)
        zero = jnp.zeros_like(q_b)
        pv = []
        for e in range(2):
            qm = jnp.where(k_even if e == 0 else ~k_even, q_b, zero)
            scores = lax.dot_general(qm, k_b, NT, preferred_element_type=F32) * dec_ref[2 * j + e]
            pv.append(dot(scores.astype(BF16), v_b))
        st = state_ref[j]
        o = jnp.where(v_even, pv[0], pv[1]) + dot((q * qdec_ref[:, cols]).astype(BF16), st.astype(BF16))
        upd = lax.dot_general((k * kdec_ref[:, cols]).astype(BF16), v_b, TN, preferred_element_type=F32)
        state_ref[j] = st * sdec_ref[j] + upd * own
        ms = dot((o * o).astype(BF16), hsum) * (1.0 / HEAD_DIM)
        gate = g_ref[0, :, cols]
        gate = gate * _sigmoid(gate)
        o_ref[0, :, cols] = (o * lax.rsqrt(ms + NORM_EPS) * ng_ref[:, cols] * gate).astype(o_ref.dtype)


def _retention_tables(T):
    log_g = jnp.log(1.0 - jnp.exp2(-5.0 - jnp.arange(RET_HEADS, dtype=F32)))
    n = jnp.arange(T)
    visible = (n[None, :] <= n[:, None]) | ((n[None, :] // RET_CHUNK) == (n[:, None] // RET_CHUNK))
    dist = jnp.abs(n[:, None] - n[None, :]).astype(F32)
    dec = jnp.where(visible[None], jnp.exp(dist[None] * log_g[:, None, None]), 0.0)
    lane = np.arange(RET_W)
    k_head = 2 * (lane // 128) + ((lane % HEAD_DIM) // (HEAD_DIM // 2))
    v_head = lane // HEAD_DIM
    pos = jnp.arange(T, dtype=F32)[:, None]
    qdec = jnp.exp((pos + 1.0) * log_g[k_head][None, :])
    kdec = jnp.exp((T - 1.0 - pos) * log_g[k_head][None, :])
    block_decay = jnp.exp(T * log_g)
    sdec = jnp.broadcast_to(block_decay[k_head].reshape(RET_HEADS // 2, 128, 1), (RET_HEADS // 2, 128, 128))
    own = jnp.asarray(k_head[:128, None] == v_head[None, :128], F32)
    hsum = jnp.asarray(v_head[:128, None] == v_head[None, :128], BF16)
    return dec, qdec, kdec, sdec, own, hsum


def _retention(proj, cos4, sin4, norm_g, layer, T=256):
    B, S, _ = proj.shape
    dec, qdec, kdec, sdec, own, hsum = _retention_tables(T)
    blk = lambda c: pl.BlockSpec((1, T, RET_W), lambda b, s, c=c: (b, s, c))
    const = lambda a: pl.BlockSpec(a.shape, lambda b, s: (0,) * a.ndim)
    return pl.pallas_call(
        functools.partial(_retention_kernel, T=T),
        out_shape=jax.ShapeDtypeStruct((B, S, RET_W), BF16),
        grid=(B, S // T),
        in_specs=[
            blk(0), blk(1), blk(2), blk(3),
            pl.BlockSpec((T, 128), lambda b, s: (s, 0)),
            pl.BlockSpec((T, 128), lambda b, s: (s, 0)),
            pl.BlockSpec((None, 1, RET_W), lambda b, s: (layer, 0, 0)),
            const(dec), const(qdec), const(kdec), const(sdec), const(own), const(hsum),
        ],
        out_specs=pl.BlockSpec((1, T, RET_W), lambda b, s: (b, s, 0)),
        scratch_shapes=[pltpu.VMEM((RET_HEADS // 2, 128, 128), F32)],
        compiler_params=pltpu.CompilerParams(
            dimension_semantics=("parallel", "arbitrary"), vmem_limit_bytes=VMEM_LIMIT),
        name="retention",
    )(proj, proj, proj, proj, cos4, sin4, norm_g, dec, qdec, kdec, sdec, own, hsum)


def _stick_kernel(q_ref, kv_ref, o_ref, *, TB):
    i = pl.program_id(1)
    row = lax.broadcasted_iota(jnp.int32, (TB, TB), 0)
    col = lax.broadcasted_iota(jnp.int32, (TB, TB), 1)
    strict = col < row
    rev_incl = (row >= col).astype(BF16)
    low_lanes = lax.broadcasted_iota(jnp.int32, (TB, 2 * HEAD_DIM), 1) < HEAD_DIM

    qm = []
    for h in range(SB_HEADS):
        qp = q_ref[0, :, (h // 2) * 128:(h // 2 + 1) * 128] * (HEAD_DIM ** -0.5)
        qm.append(jnp.where(low_lanes if h % 2 == 0 else ~low_lanes, qp, jnp.zeros_like(qp)))

    def block(n, o, acc, diagonal):
        rows = pl.ds(pl.multiple_of(n * TB, TB), TB)
        heads = range(SB_HEADS)
        kp = [kv_ref[0, rows, j * 128:(j + 1) * 128] for j in range(SB_HEADS // 2)]
        vp = [kv_ref[0, rows, SB_W + j * 128:SB_W + (j + 1) * 128] for j in range(SB_HEADS // 2)]
        z = [lax.dot_general(qm[h], kp[h // 2], NT, preferred_element_type=F32) for h in heads]
        ls = [-_softplus(z[h]) for h in heads]
        if diagonal:
            ls = [jnp.where(strict, ls[h], 0.0) for h in heads]
        cum = [jnp.dot(ls[h].astype(BF16), rev_incl, preferred_element_type=F32) for h in heads]
        if acc is None:
            w = [jnp.exp(z[h] + cum[h]) for h in heads]
        else:
            w = [jnp.exp(z[h] + cum[h] + acc[h]) for h in heads]
        if diagonal:
            w = [jnp.where(strict, w[h], 0.0) for h in heads]
        pv = [jnp.dot(w[h].astype(BF16), vp[h // 2], preferred_element_type=F32) for h in heads]
        tot = [jnp.sum(ls[h], axis=-1, keepdims=True) for h in heads]
        new_acc = tot if acc is None else [acc[h] + tot[h] for h in heads]
        upd = [jnp.where(low_lanes, pv[2 * j], pv[2 * j + 1]) for j in range(SB_HEADS // 2)]
        new_o = upd if o is None else [o[j] + upd[j] for j in range(SB_HEADS // 2)]
        return tuple(new_o), tuple(new_acc)

    def top_of(acc):
        return functools.reduce(jnp.maximum, [jnp.max(a) for a in acc])

    o, acc = block(i, None, None, True)

    def cond(c):
        n, _, _, top = c
        return jnp.logical_and(n >= 0, top > SB_SKIP_LOG)

    def body(c):
        n, o, acc, _ = c
        o, acc = block(n, o, acc, False)
        return n - 1, o, acc, top_of(acc)

    _, o, _, _ = lax.while_loop(cond, body, (i - 1, o, acc, top_of(acc)))
    for j in range(SB_HEADS // 2):
        o_ref[0, :, j * 128:(j + 1) * 128] = o[j].astype(o_ref.dtype)


def _stick_breaking(proj_sb, TB=SB_BLOCK):
    B, S, _ = proj_sb.shape
    return pl.pallas_call(
        functools.partial(_stick_kernel, TB=TB),
        out_shape=jax.ShapeDtypeStruct((B, S, SB_W), BF16),
        grid=(B, S // TB),
        in_specs=[
            pl.BlockSpec((1, TB, SB_W), lambda b, i: (b, i, 2)),
            pl.BlockSpec((1, S, 2 * SB_W), lambda b, i: (b, 0, 0)),
        ],
        out_specs=pl.BlockSpec((1, TB, SB_W), lambda b, i: (b, i, 0)),
        compiler_params=pltpu.CompilerParams(
            dimension_semantics=("parallel", "arbitrary"), vmem_limit_bytes=VMEM_LIMIT),
        name="stick_breaking",
    )(proj_sb, proj_sb)


def _rwkv_kernel(z_ref, mu_ref, w0_ref, wup_ref, a0_ref, aup_ref, gup_ref, kk_ref, ka_ref, rk_ref,
                 lnw_ref, lnb_ref, o_ref,
                 state_ref, prev_ref, sola_s, solv_s, mrb_s, yfix_s, sfix_s, y_s, *, TS):
    C = RWKV_CHUNK
    W = RWKV_W
    NC = TS // C

    @pl.when(pl.program_id(1) == 0)
    def _():
        state_ref[...] = jnp.zeros_like(state_ref)
        prev_ref[...] = jnp.zeros_like(prev_ref)

    z = z_ref[0]
    first_row = lax.broadcasted_iota(jnp.int32, (TS, 1), 0) == 0
    z_prev = jnp.where(first_row, prev_ref[...], pltpu.roll(z, 1, 0))
    prev_ref[...] = z[TS - 1:TS, :]
    z = z + mu_ref[...] * (z_prev - z)

    r, k, v = z[:, 0:W], z[:, W:2 * W], z[:, 2 * W:3 * W]
    wd = z[:, 3 * W:3 * W + DECAY_LORA]
    ad = z[:, 3 * W + DECAY_LORA:3 * W + DECAY_LORA + AAA_LORA]
    gd = z[:, 3 * W + DECAY_LORA + AAA_LORA:]

    gi = lax.broadcasted_iota(jnp.int32, (W, W), 0) // HEAD_DIM
    gj = lax.broadcasted_iota(jnp.int32, (W, W), 1) // HEAD_DIM
    same_head = (gi == gj).astype(BF16)

    def head_sum(x):
        return jnp.dot(x.astype(BF16), same_head, preferred_element_type=F32)

    log_w = -_softplus(-(w0_ref[...] + _mm(jnp.tanh(wd), wup_ref[...]))) - 0.5
    lw = -jnp.exp(log_w)
    a = _sigmoid(a0_ref[...] + _mm(ad, aup_ref[...]))
    g = _mm(_sigmoid(gd), gup_ref[...])
    kk = k * kk_ref[...]
    kk = kk / jnp.maximum(jnp.sqrt(head_sum(kk * kk)), 1e-12)
    k_mod = k * (1.0 + (a - 1.0) * ka_ref[...])
    a_vec = -kk
    b_vec = kk * a

    ti = lax.broadcasted_iota(jnp.int32, (C, C), 0)
    tj = lax.broadcasted_iota(jnp.int32, (C, C), 1)
    strict = tj < ti
    incl = tj <= ti
    tri_incl = incl.astype(F32)

    systems = [(c, h) for c in range(NC) for h in range(RWKV_HEADS)]
    head = lambda t, h: t[:, h * HEAD_DIM:(h + 1) * HEAD_DIM]
    dot = functools.partial(jnp.dot, preferred_element_type=F32)
    dot_nt = functools.partial(lax.dot_general, dimension_numbers=NT, preferred_element_type=F32)
    dot_tn = functools.partial(lax.dot_general, dimension_numbers=TN, preferred_element_type=F32)

    at_f, at_b, rt_b, bt_b, kt_b, bh_b, kh_b, v_b, p_end = [], [], [], [], [], [], [], [], []
    for c in range(NC):
        rows = slice(c * C, (c + 1) * C)
        lw_c = lw[rows]
        cs = _mm(tri_incl, lw_c, exact=True)
        e_incl = jnp.exp(cs)
        e_inv = jnp.exp(-cs)
        p_c = e_incl[C - 1:C, :]
        at = a_vec[rows] * jnp.exp(cs - lw_c)
        bt = b_vec[rows] * e_inv
        kt = k_mod[rows] * e_inv
        at_f.append(at)
        at_b.append(at.astype(BF16))
        rt_b.append((r[rows] * e_incl).astype(BF16))
        bt_b.append(bt.astype(BF16))
        kt_b.append(kt.astype(BF16))
        bh_b.append((bt * p_c).astype(BF16))
        kh_b.append((kt * p_c).astype(BF16))
        v_b.append(v[rows].astype(BF16))
        p_end.append(p_c)

    vh = [head(v_b[c], h) for c, h in systems]
    gram = [dot_nt(jnp.concatenate([head(at_b[c], h), head(rt_b[c], h)], axis=0),
                   jnp.concatenate([head(bt_b[c], h), head(kt_b[c], h)], axis=0)) for c, h in systems]
    a_ab = [jnp.where(strict, gm[0:C, 0:C], 0.0).astype(BF16) for gm in gram]
    a_ak = [jnp.where(strict, gm[0:C, C:2 * C], 0.0).astype(BF16) for gm in gram]
    m_rk = [jnp.where(incl, gm[C:2 * C, C:2 * C], 0.0).astype(BF16) for gm in gram]
    for i, gm in enumerate(gram):
        mrb_s[i] = jnp.where(incl, gm[C:2 * C, 0:C], 0.0)
    akv = [dot(a_ak[i], vh[i]) for i in range(len(systems))]
    sol = [jnp.concatenate([head(at_f[c], h), akv[i]], axis=1) for i, (c, h) in enumerate(systems)]
    pw = a_ab
    for level in range(int(math.log2(C))):
        if level > 0:
            pw = [dot(m, m).astype(BF16) for m in pw]
        sol = [x + dot(m, x.astype(BF16)) for m, x in zip(pw, sol)]
    for i, (c, h) in enumerate(systems):
        sola_s[i] = sol[i][:, 0:HEAD_DIM]
        solv_s[i] = sol[i][:, HEAD_DIM:]
        yfix_s[i] = dot(m_rk[i], vh[i])
        sfix_s[i] = dot_tn(vh[i], head(kh_b[c], h))

    heads = range(RWKV_HEADS)
    for c in range(NC):
        rows = slice(c * C, (c + 1) * C)
        base = c * RWKV_HEADS
        st = [state_ref[h] for h in heads]
        st_b = [s.astype(BF16) for s in st]
        u = [dot_nt(sola_s[base + h].astype(BF16), st_b[h]) + solv_s[base + h] for h in heads]
        u_b = [x.astype(BF16) for x in u]
        ys = [dot_nt(head(rt_b[c], h), st_b[h]) for h in heads]
        yu = [dot(mrb_s[base + h].astype(BF16), u_b[h]) for h in heads]
        su = [dot_tn(u_b[h], head(bh_b[c], h)) for h in heads]
        for h in heads:
            y_s[rows, h * HEAD_DIM:(h + 1) * HEAD_DIM] = ys[h] + yu[h] + yfix_s[base + h]
            state_ref[h] = st[h] * head(p_end[c], h) + su[h] + sfix_s[base + h]

    y = y_s[...]
    mean = head_sum(y) * (1.0 / HEAD_DIM)
    d = y - mean
    var = head_sum(d * d) * (1.0 / HEAD_DIM)
    y = d * lax.rsqrt(var + RWKV_GN_EPS) * lnw_ref[...] + lnb_ref[...]
    bonus = head_sum(r * k_mod * rk_ref[...]) * v
    o_ref[0] = ((y + bonus) * g).astype(o_ref.dtype)


def _rwkv7(proj, params, layer, TS=256):
    B, S, _ = proj.shape
    W = RWKV_W
    n_sys = (TS // RWKV_CHUNK) * RWKV_HEADS
    per_layer = lambda a: pl.BlockSpec((None,) + a.shape[1:], lambda b, s: (layer, 0, 0))
    per_sys = lambda: pltpu.VMEM((n_sys, RWKV_CHUNK, HEAD_DIM), F32)
    return pl.pallas_call(
        functools.partial(_rwkv_kernel, TS=TS),
        out_shape=jax.ShapeDtypeStruct((B, S, W), BF16),
        grid=(B, S // TS),
        in_specs=[
            pl.BlockSpec((1, TS, RWKV_IN), lambda b, s: (b, s, COL_RWKV // RWKV_IN)),
        ] + [per_layer(a) for a in params],
        out_specs=pl.BlockSpec((1, TS, W), lambda b, s: (b, s, 0)),
        scratch_shapes=[
            pltpu.VMEM((RWKV_HEADS, HEAD_DIM, HEAD_DIM), F32),
            pltpu.VMEM((1, RWKV_IN), F32),
            per_sys(), per_sys(), per_sys(), per_sys(), per_sys(),
            pltpu.VMEM((TS, W), F32),
        ],
        compiler_params=pltpu.CompilerParams(
            dimension_semantics=("parallel", "arbitrary"), vmem_limit_bytes=VMEM_LIMIT),
        name="rwkv7",
    )(proj, *params)


def _post_kernel(h_ref, oret_ref, osb_ref, orw_ref, p_ref, wo_ref, gm_ref, w1_ref, w2_ref,
                 gp_ref, wpg_ref, wpe_ref, out_ref, *, ff_chunk):
    h = h_ref[...]
    h = h + jnp.dot(oret_ref[...], wo_ref[0:RET_W, :], preferred_element_type=F32)
    h = h + jnp.dot(osb_ref[...], wo_ref[RET_W:RET_W + SB_W, :], preferred_element_type=F32)
    h = h + jnp.dot(orw_ref[...], wo_ref[RET_W + SB_W:, :], preferred_element_type=F32)

    n = _rms(h, gm_ref[...]).astype(BF16)
    out_ref[...] = h
    for c in range(D_FF // ff_chunk):
        cols = slice(c * ff_chunk, (c + 1) * ff_chunk)
        u = jnp.dot(n, w1_ref[:, cols], preferred_element_type=F32)
        u = jnp.square(jnp.maximum(u, 0.0)).astype(BF16)
        out_ref[...] += jnp.dot(u, w2_ref[cols, :], preferred_element_type=F32)
    h = out_ref[...]

    n = _rms(h, gp_ref[...]).astype(BF16)
    gate = _sigmoid(jnp.dot(n, wpg_ref[...], preferred_element_type=F32))
    pe = jnp.dot(p_ref[...].astype(BF16), wpe_ref[...], preferred_element_type=F32)
    out_ref[...] = h + gate * pe


def _post(h, o_ret, o_sb, o_rw, p, w_o, g_mlp, w1, w2, g_ple, w_pg, w_pe, layer, tm=512, ff_chunk=1024):
    M, D = h.shape
    rows = lambda n: pl.BlockSpec((tm, n), lambda i: (i, 0))
    const = lambda a: pl.BlockSpec((None,) + a.shape[1:], lambda i: (layer, 0, 0), pipeline_mode=pl.Buffered(1))
    return pl.pallas_call(
        functools.partial(_post_kernel, ff_chunk=ff_chunk),
        out_shape=jax.ShapeDtypeStruct((M, D), F32),
        grid=(M // tm,),
        in_specs=[
            rows(D), rows(RET_W), rows(SB_W), rows(RWKV_W),
            pl.BlockSpec((None, tm, PLE_DIM), lambda i: (layer, i, 0)),
            const(w_o), const(g_mlp), const(w1), const(w2), const(g_ple), const(w_pg), const(w_pe),
        ],
        out_specs=rows(D),
        compiler_params=pltpu.CompilerParams(
            dimension_semantics=("parallel",), vmem_limit_bytes=VMEM_LIMIT),
        name="post",
    )(h, o_ret, o_sb, o_rw, p, w_o, g_mlp, w1, w2, g_ple, w_pg, w_pe)


def _final_norm_kernel(h_ref, g_ref, o_ref):
    o_ref[...] = _rms(h_ref[...], g_ref[...])


def _final_norm(h, g, tm=1024):
    M, D = h.shape
    return pl.pallas_call(
        _final_norm_kernel,
        out_shape=jax.ShapeDtypeStruct((M, D), F32),
        grid=(M // tm,),
        in_specs=[pl.BlockSpec((tm, D), lambda i: (i, 0)), pl.BlockSpec((1, D), lambda i: (0, 0))],
        out_specs=pl.BlockSpec((tm, D), lambda i: (i, 0)),
        compiler_params=pltpu.CompilerParams(dimension_semantics=("parallel",)),
        name="final_norm",
    )(h, g.reshape(1, D))


def _proj_column_order():
    half = HEAD_DIM // 2
    pair = np.concatenate([np.arange(0, half), HEAD_DIM + np.arange(0, half),
                           np.arange(half, HEAD_DIM), HEAD_DIM + np.arange(half, HEAD_DIM)])
    qk_perm = np.concatenate([j * 128 + pair for j in range(RET_HEADS // 2)])
    ret = np.concatenate([qk_perm, RET_W + qk_perm, np.arange(2 * RET_W, RET_IN)])
    sbq = np.arange(RET_IN, RET_IN + SB_W)
    sbk = np.arange(RET_IN + SB_W, RET_IN + 2 * SB_W)
    sbv = np.arange(RET_IN + 2 * SB_W, RET_IN + 3 * SB_W)
    return np.concatenate([ret, np.arange(RET_IN + SB_IN, IN_W), sbk, sbv, sbq])


def _rope_tables(S):
    pos = jnp.arange(S, dtype=F32)
    inv_freq = 1.0 / (ROPE_BASE ** jnp.linspace(0.0, 1.0, HEAD_DIM // 2, dtype=F32))
    ang = pos[:, None] * inv_freq[None, :]
    cos, sin = jnp.cos(ang), jnp.sin(ang)
    return jnp.concatenate([cos, cos, cos, cos], axis=-1), jnp.concatenate([-sin, -sin, sin, sin], axis=-1)


def kernel(x, p, norm_mix_g, norm_mlp_g, norm_ple_g, w_in, ret_norm_g, rwkv_mu, rwkv_w0, rwkv_w_up, rwkv_a0, rwkv_a_up, rwkv_g_up, rwkv_k_k, rwkv_k_a, rwkv_r_k, rwkv_ln_w, rwkv_ln_b, w_o, w_mlp_in, w_mlp_out, w_pe, w_pg, final_norm_g):
    B, S, D = x.shape
    depth = w_in.shape[0]
    M = B * S
    w_in_p = jnp.take(w_in, jnp.asarray(_proj_column_order()), axis=2).astype(BF16)
    w_o_b = w_o.astype(BF16)
    w1_b = w_mlp_in.astype(BF16)
    w2_b = w_mlp_out.astype(BF16)
    w_pg_b = w_pg.astype(BF16)
    w_pe_b = w_pe.astype(BF16)
    cos4, sin4 = _rope_tables(S)

    vec = lambda a: a.reshape(depth, 1, a.shape[-1])
    g_mix, g_mlp, g_ple, g_ret = vec(norm_mix_g), vec(norm_mlp_g), vec(norm_ple_g), vec(ret_norm_g)
    rwkv_params = (vec(rwkv_mu), vec(rwkv_w0), rwkv_w_up, vec(rwkv_a0), rwkv_a_up, rwkv_g_up,
                   vec(rwkv_k_k), vec(rwkv_k_a), vec(rwkv_r_k), vec(rwkv_ln_w), vec(rwkv_ln_b))
    p_rows = p.reshape(depth, M, PLE_DIM)

    h = x.reshape(M, D)
    for i in range(depth):
        proj, proj_sb = _norm_proj(h, g_mix, w_in_p, i)
        proj = proj.reshape(B, S, MAIN_W)
        o_ret = _retention(proj, cos4, sin4, g_ret, i)
        o_sb = _stick_breaking(proj_sb.reshape(B, S, SB_IN))
        o_rw = _rwkv7(proj, rwkv_params, i)
        h = _post(h, o_ret.reshape(M, RET_W), o_sb.reshape(M, SB_W), o_rw.reshape(M, RWKV_W), p_rows,
                  w_o_b, g_mlp, w1_b, w2_b, g_ple, w_pg_b, w_pe_b, i)
    return _final_norm(h, final_norm_g).reshape(B, S, D)
```

```python
import functools
import math

import numpy as np
import jax
import jax.numpy as jnp
from jax import lax
from jax.experimental import pallas as pl
from jax.experimental.pallas import tpu as pltpu

F32 = jnp.float32
BF16 = jnp.bfloat16

D_MODEL = 1024
HEAD_DIM = 64
RET_W = 512
SB_W = 256
RWKV_W = 256
RET_HEADS = 8
SB_HEADS = 4
RWKV_HEADS = 4
DECAY_LORA = 64
AAA_LORA = 64
GATE_LORA = 128
RWKV_IN = 3 * RWKV_W + DECAY_LORA + AAA_LORA + GATE_LORA
RET_IN = 4 * RET_W
SB_IN = 3 * SB_W
IN_W = RET_IN + SB_IN + RWKV_IN
D_FF = 4 * D_MODEL
PLE_DIM = 256
RET_CHUNK = 64
SB_BLOCK = 256
ROPE_BASE = 10000.0
NORM_EPS = 1e-6
RWKV_GN_EPS = 64e-5

COL_RET = 0
COL_RWKV = RET_IN
MAIN_W = RET_IN + RWKV_IN

RET_LOG_G = tuple(math.log(1.0 - 2.0 ** (-5.0 - h)) for h in range(RET_HEADS))
SB_SKIP_LOG = -104.0
RWKV_CHUNK = 64

VMEM_LIMIT = 56 * 1024 * 1024

NT = (((1,), (1,)), ((), ()))
TN = (((0,), (0,)), ((), ()))


def _mm(a, b, dims=None, exact=False):
    if dims is None:
        dims = (((a.ndim - 1,), (0,)), ((), ()))
    if exact:
        return lax.dot_general(a, b, dims, precision=lax.Precision.HIGHEST, preferred_element_type=F32)
    return lax.dot_general(a.astype(BF16), b.astype(BF16), dims, preferred_element_type=F32)


def _rms(x, g):
    return x * lax.rsqrt(jnp.mean(x * x, axis=-1, keepdims=True) + NORM_EPS) * g


def _softplus(z):
    return jnp.maximum(z, 0.0) + jnp.log(1.0 + jnp.exp(-jnp.abs(z)))


def _sigmoid(z):
    return 1.0 / (1.0 + jnp.exp(-z))


def _norm_proj_kernel(h_ref, g_ref, w_ref, o_ref, sb_ref):
    n = _rms(h_ref[...], g_ref[...]).astype(BF16)
    o_ref[...] = jnp.dot(n, w_ref[:, 0:MAIN_W], preferred_element_type=F32)
    sb_ref[...] = jnp.dot(n, w_ref[:, MAIN_W:], preferred_element_type=F32).astype(BF16)


def _norm_proj(h, g, w, layer, tm=512):
    M, D = h.shape
    N = w.shape[2]
    return pl.pallas_call(
        _norm_proj_kernel,
        out_shape=(jax.ShapeDtypeStruct((M, MAIN_W), F32), jax.ShapeDtypeStruct((M, SB_IN), BF16)),
        grid=(M // tm,),
        in_specs=[
            pl.BlockSpec((tm, D), lambda i: (i, 0)),
            pl.BlockSpec((None, 1, D), lambda i: (layer, 0, 0)),
            pl.BlockSpec((None, D, N), lambda i: (layer, 0, 0)),
        ],
        out_specs=(pl.BlockSpec((tm, MAIN_W), lambda i: (i, 0)), pl.BlockSpec((tm, SB_IN), lambda i: (i, 0))),
        compiler_params=pltpu.CompilerParams(
            dimension_semantics=("parallel",), vmem_limit_bytes=VMEM_LIMIT),
        name="norm_proj",
    )(h, g, w)


def _retention_kernel(q_ref, k_ref, v_ref, g_ref, cos_ref, sin_ref, ng_ref, dec_ref, qdec_ref, kdec_ref,
                      sdec_ref, own_ref, hsum_ref, o_ref, state_ref, *, T):
    @pl.when(pl.program_id(1) == 0)
    def _():
        state_ref[...] = jnp.zeros_like(state_ref)

    cos = cos_ref[...]
    sin = sin_ref[...]
    lane = lax.broadcasted_iota(jnp.int32, (T, 128), 1)
    k_even = (lane % HEAD_DIM) < (HEAD_DIM // 2)
    v_even = lane < HEAD_DIM
    own = own_ref[...]
    hsum = hsum_ref[...]
    dot = functools.partial(jnp.dot, preferred_element_type=F32)

    for j in range(RET_HEADS // 2):
        cols = slice(j * 128, (j + 1) * 128)
        q = q_ref[0, :, cols]
        k = k_ref[0, :, cols]
        q = q * cos + pltpu.roll(q, HEAD_DIM, 1) * sin
        k = (k * cos + pltpu.roll(k, HEAD_DIM, 1) * sin) * (HEAD_DIM ** -0.5)
        v_b = v_ref[0, :, cols].astype(BF16)
        k_b = k.astype(BF16)
        q_b = q.astype(BF16)
        zero = jnp.zeros_like(q_b)
        pv = []
        for e in range(2):
            qm = jnp.where(k_even if e == 0 else ~k_even, q_b, zero)
            scores = lax.dot_general(qm, k_b, NT, preferred_element_type=F32) * dec_ref[2 * j + e]
            pv.append(dot(scores.astype(BF16), v_b))
        st = state_ref[j]
        o = jnp.where(v_even, pv[0], pv[1]) + dot((q * qdec_ref[:, cols]).astype(BF16), st.astype(BF16))
        upd = lax.dot_general((k * kdec_ref[:, cols]).astype(BF16), v_b, TN, preferred_element_type=F32)
        state_ref[j] = st * sdec_ref[j] + upd * own
        ms = dot((o * o).astype(BF16), hsum) * (1.0 / HEAD_DIM)
        gate = g_ref[0, :, cols]
        gate = gate * _sigmoid(gate)
        o_ref[0, :, cols] = (o * lax.rsqrt(ms + NORM_EPS) * ng_ref[:, cols] * gate).astype(o_ref.dtype)


def _retention_tables(T):
    log_g = jnp.log(1.0 - jnp.exp2(-5.0 - jnp.arange(RET_HEADS, dtype=F32)))
    n = jnp.arange(T)
    visible = (n[None, :] <= n[:, None]) | ((n[None, :] // RET_CHUNK) == (n[:, None] // RET_CHUNK))
    dist = jnp.abs(n[:, None] - n[None, :]).astype(F32)
    dec = jnp.where(visible[None], jnp.exp(dist[None] * log_g[:, None, None]), 0.0)
    lane = np.arange(RET_W)
    k_head = 2 * (lane // 128) + ((lane % HEAD_DIM) // (HEAD_DIM // 2))
    v_head = lane // HEAD_DIM
    pos = jnp.arange(T, dtype=F32)[:, None]
    qdec = jnp.exp((pos + 1.0) * log_g[k_head][None, :])
    kdec = jnp.exp((T - 1.0 - pos) * log_g[k_head][None, :])
    block_decay = jnp.exp(T * log_g)
    sdec = jnp.broadcast_to(block_decay[k_head].reshape(RET_HEADS // 2, 128, 1), (RET_HEADS // 2, 128, 128))
    own = jnp.asarray(k_head[:128, None] == v_head[None, :128], F32)
    hsum = jnp.asarray(v_head[:128, None] == v_head[None, :128], BF16)
    return dec, qdec, kdec, sdec, own, hsum


def _retention(proj, cos4, sin4, norm_g, layer, T=256):
    B, S, _ = proj.shape
    dec, qdec, kdec, sdec, own, hsum = _retention_tables(T)
    blk = lambda c: pl.BlockSpec((1, T, RET_W), lambda b, s, c=c: (b, s, c))
    const = lambda a: pl.BlockSpec(a.shape, lambda b, s: (0,) * a.ndim)
    return pl.pallas_call(
        functools.partial(_retention_kernel, T=T),
        out_shape=jax.ShapeDtypeStruct((B, S, RET_W), BF16),
        grid=(B, S // T),
        in_specs=[
            blk(0), blk(1), blk(2), blk(3),
            pl.BlockSpec((T, 128), lambda b, s: (s, 0)),
            pl.BlockSpec((T, 128), lambda b, s: (s, 0)),
            pl.BlockSpec((None, 1, RET_W), lambda b, s: (layer, 0, 0)),
            const(dec), const(qdec), const(kdec), const(sdec), const(own), const(hsum),
        ],
        out_specs=pl.BlockSpec((1, T, RET_W), lambda b, s: (b, s, 0)),
        scratch_shapes=[pltpu.VMEM((RET_HEADS // 2, 128, 128), F32)],
        compiler_params=pltpu.CompilerParams(
            dimension_semantics=("parallel", "arbitrary"), vmem_limit_bytes=VMEM_LIMIT),
        name="retention",
    )(proj, proj, proj, proj, cos4, sin4, norm_g, dec, qdec, kdec, sdec, own, hsum)


def _stick_kernel(q_ref, kv_ref, o_ref, *, TB):
    i = pl.program_id(1)
    row = lax.broadcasted_iota(jnp.int32, (TB, TB), 0)
    col = lax.broadcasted_iota(jnp.int32, (TB, TB), 1)
    strict = col < row
    rev_incl = (row >= col).astype(BF16)
    low_lanes = lax.broadcasted_iota(jnp.int32, (TB, 2 * HEAD_DIM), 1) < HEAD_DIM

    qm = []
    for h in range(SB_HEADS):
        qp = q_ref[0, :, (h // 2) * 128:(h // 2 + 1) * 128] * (HEAD_DIM ** -0.5)
        qm.append(jnp.where(low_lanes if h % 2 == 0 else ~low_lanes, qp, jnp.zeros_like(qp)))

    def block(n, o, acc, diagonal):
        rows = pl.ds(pl.multiple_of(n * TB, TB), TB)
        heads = range(SB_HEADS)
        kp = [kv_ref[0, rows, j * 128:(j + 1) * 128] for j in range(SB_HEADS // 2)]
        vp = [kv_ref[0, rows, SB_W + j * 128:SB_W + (j + 1) * 128] for j in range(SB_HEADS // 2)]
        z = [lax.dot_general(qm[h], kp[h // 2], NT, preferred_element_type=F32) for h in heads]
        ls = [-_softplus(z[h]) for h in heads]
        if diagonal:
            ls = [jnp.where(strict, ls[h], 0.0) for h in heads]
        cum = [jnp.dot(ls[h].astype(BF16), rev_incl, preferred_element_type=F32) for h in heads]
        if acc is None:
            w = [jnp.exp(z[h] + cum[h]) for h in heads]
        else:
            w = [jnp.exp(z[h] + cum[h] + acc[h]) for h in heads]
        if diagonal:
            w = [jnp.where(strict, w[h], 0.0) for h in heads]
        pv = [jnp.dot(w[h].astype(BF16), vp[h // 2], preferred_element_type=F32) for h in heads]
        tot = [jnp.sum(ls[h], axis=-1, keepdims=True) for h in heads]
        new_acc = tot if acc is None else [acc[h] + tot[h] for h in heads]
        upd = [jnp.where(low_lanes, pv[2 * j], pv[2 * j + 1]) for j in range(SB_HEADS // 2)]
        new_o = upd if o is None else [o[j] + upd[j] for j in range(SB_HEADS // 2)]
        return tuple(new_o), tuple(new_acc)

    def top_of(acc):
        return functools.reduce(jnp.maximum, [jnp.max(a) for a in acc])

    o, acc = block(i, None, None, True)

    def cond(c):
        n, _, _, top = c
        return jnp.logical_and(n >= 0, top > SB_SKIP_LOG)

    def body(c):
        n, o, acc, _ = c
        o, acc = block(n, o, acc, False)
        return n - 1, o, acc, top_of(acc)

    _, o, _, _ = lax.while_loop(cond, body, (i - 1, o, acc, top_of(acc)))
    for j in range(SB_HEADS // 2):
        o_ref[0, :, j * 128:(j + 1) * 128] = o[j].astype(o_ref.dtype)


def _stick_breaking(proj_sb, TB=SB_BLOCK):
    B, S, _ = proj_sb.shape
    return pl.pallas_call(
        functools.partial(_stick_kernel, TB=TB),
        out_shape=jax.ShapeDtypeStruct((B, S, SB_W), BF16),
        grid=(B, S // TB),
        in_specs=[
            pl.BlockSpec((1, TB, SB_W), lambda b, i: (b, i, 2)),
            pl.BlockSpec((1, S, 2 * SB_W), lambda b, i: (b, 0, 0)),
        ],
        out_specs=pl.BlockSpec((1, TB, SB_W), lambda b, i: (b, i, 0)),
        compiler_params=pltpu.CompilerParams(
            dimension_semantics=("parallel", "arbitrary"), vmem_limit_bytes=VMEM_LIMIT),
        name="stick_breaking",
    )(proj_sb, proj_sb)


def _rwkv_kernel(z_ref, mu_ref, w0_ref, wup_ref, a0_ref, aup_ref, gup_ref, kk_ref, ka_ref, rk_ref,
                 lnw_ref, lnb_ref, o_ref,
                 state_ref, prev_ref, lhs_s, solv_s, mrb_s, yfix_s, sfix_s, y_s, *, TS):
    C = RWKV_CHUNK
    W = RWKV_W
    NC = TS // C

    @pl.when(pl.program_id(1) == 0)
    def _():
        state_ref[...] = jnp.zeros_like(state_ref)
        prev_ref[...] = jnp.zeros_like(prev_ref)

    z = z_ref[0]
    first_row = lax.broadcasted_iota(jnp.int32, (TS, 1), 0) == 0
    z_prev = jnp.where(first_row, prev_ref[...], pltpu.roll(z, 1, 0))
    prev_ref[...] = z[TS - 1:TS, :]
    z = z + mu_ref[...] * (z_prev - z)

    r, k, v = z[:, 0:W], z[:, W:2 * W], z[:, 2 * W:3 * W]
    wd = z[:, 3 * W:3 * W + DECAY_LORA]
    ad = z[:, 3 * W + DECAY_LORA:3 * W + DECAY_LORA + AAA_LORA]
    gd = z[:, 3 * W + DECAY_LORA + AAA_LORA:]

    gi = lax.broadcasted_iota(jnp.int32, (W, W), 0) // HEAD_DIM
    gj = lax.broadcasted_iota(jnp.int32, (W, W), 1) // HEAD_DIM
    same_head = (gi == gj).astype(BF16)

    def head_sum(x):
        return jnp.dot(x.astype(BF16), same_head, preferred_element_type=F32)

    log_w = -_softplus(-(w0_ref[...] + _mm(jnp.tanh(wd), wup_ref[...]))) - 0.5
    lw = -jnp.exp(log_w)
    a = _sigmoid(a0_ref[...] + _mm(ad, aup_ref[...]))
    g = _mm(_sigmoid(gd), gup_ref[...])
    kk = k * kk_ref[...]
    kk = kk / jnp.maximum(jnp.sqrt(head_sum(kk * kk)), 1e-12)
    k_mod = k * (1.0 + (a - 1.0) * ka_ref[...])
    a_vec = -kk
    b_vec = kk * a

    ti = lax.broadcasted_iota(jnp.int32, (2 * C, C), 0)
    tj = lax.broadcasted_iota(jnp.int32, (2 * C, C), 1)
    stacked_tri = tj <= jnp.where(ti < C, ti - 1, ti - C)
    tri_incl = (lax.broadcasted_iota(jnp.int32, (C, C), 1) <= lax.broadcasted_iota(jnp.int32, (C, C), 0)).astype(BF16)

    systems = [(c, h) for c in range(NC) for h in range(RWKV_HEADS)]
    head = lambda t, h: t[:, h * HEAD_DIM:(h + 1) * HEAD_DIM]
    dot = functools.partial(jnp.dot, preferred_element_type=F32)
    dot_nt = functools.partial(lax.dot_general, dimension_numbers=NT, preferred_element_type=F32)
    dot_tn = functools.partial(lax.dot_general, dimension_numbers=TN, preferred_element_type=F32)

    at_f, at_b, rt_b, bt_b, kt_b, bh_b, kh_b, v_b, p_end = [], [], [], [], [], [], [], [], []
    for c in range(NC):
        rows = slice(c * C, (c + 1) * C)
        lw_c = lw[rows]
        lw_hi = lw_c.astype(BF16)
        lw_lo = (lw_c - lw_hi.astype(F32)).astype(BF16)
        cs2 = dot(tri_incl, jnp.concatenate([lw_hi, lw_lo], axis=1))
        cs = cs2[:, 0:W] + cs2[:, W:]
        e_incl = jnp.exp(cs)
        e_inv = jnp.exp(-cs)
        p_c = e_incl[C - 1:C, :]
        at = a_vec[rows] * jnp.exp(cs - lw_c)
        bt = b_vec[rows] * e_inv
        kt = k_mod[rows] * e_inv
        at_f.append(at)
        at_b.append(at.astype(BF16))
        rt_b.append((r[rows] * e_incl).astype(BF16))
        bt_b.append(bt.astype(BF16))
        kt_b.append(kt.astype(BF16))
        bh_b.append((bt * p_c).astype(BF16))
        kh_b.append((kt * p_c).astype(BF16))
        v_b.append(v[rows].astype(BF16))
        p_end.append(p_c)

    n_sys = len(systems)
    vh = [head(v_b[c], h) for c, h in systems]
    rth = [head(rt_b[c], h) for c, h in systems]
    gram = [dot_nt(jnp.concatenate([head(at_b[c], h), rth[i]], axis=0),
                   jnp.concatenate([head(bt_b[c], h), head(kt_b[c], h)], axis=0))
            for i, (c, h) in enumerate(systems)]
    on_b = [jnp.where(stacked_tri, gm[:, 0:C], 0.0) for gm in gram]
    on_k = [jnp.where(stacked_tri, gm[:, C:2 * C], 0.0).astype(BF16) for gm in gram]
    kv = [dot(on_k[i], vh[i]) for i in range(n_sys)]
    for i in range(n_sys):
        mrb_s[i] = on_b[i][C:2 * C]
        yfix_s[i] = kv[i][C:2 * C]
    sol = [jnp.concatenate([head(at_f[c], h), kv[i][0:C]], axis=1) for i, (c, h) in enumerate(systems)]
    pw = [m[0:C].astype(BF16) for m in on_b]
    levels = int(math.log2(C))
    for level in range(levels):
        if level < levels - 1:
            both = [dot(pw[i], jnp.concatenate([sol[i].astype(BF16), pw[i], jnp.zeros_like(pw[i])], axis=1))
                    for i in range(n_sys)]
            sol = [sol[i] + both[i][:, 0:2 * HEAD_DIM] for i in range(n_sys)]
            pw = [both[i][:, 2 * HEAD_DIM:3 * HEAD_DIM].astype(BF16) for i in range(n_sys)]
        else:
            sol = [sol[i] + dot(pw[i], sol[i].astype(BF16)) for i in range(n_sys)]
    for i, (c, h) in enumerate(systems):
        lhs_s[i, 0:C, :] = sol[i][:, 0:HEAD_DIM]
        lhs_s[i, C:2 * C, :] = rth[i].astype(F32)
        solv_s[i] = sol[i][:, HEAD_DIM:]
        sfix_s[i] = dot_tn(vh[i], head(kh_b[c], h))

    heads = range(RWKV_HEADS)
    for c in range(NC):
        rows = slice(c * C, (c + 1) * C)
        base = c * RWKV_HEADS
        st = [state_ref[h] for h in heads]
        on_state = [dot_nt(lhs_s[base + h].astype(BF16), st[h].astype(BF16)) for h in heads]
        u_b = [(on_state[h][0:C] + solv_s[base + h]).astype(BF16) for h in heads]
        yu = [dot(mrb_s[base + h].astype(BF16), u_b[h]) for h in heads]
        su = [dot_tn(u_b[h], head(bh_b[c], h)) for h in heads]
        for h in heads:
            y_s[rows, h * HEAD_DIM:(h + 1) * HEAD_DIM] = on_state[h][C:2 * C] + yu[h] + yfix_s[base + h]
            state_ref[h] = st[h] * head(p_end[c], h) + su[h] + sfix_s[base + h]

    y = y_s[...]
    mean = head_sum(y) * (1.0 / HEAD_DIM)
    d = y - mean
    var = head_sum(d * d) * (1.0 / HEAD_DIM)
    y = d * lax.rsqrt(var + RWKV_GN_EPS) * lnw_ref[...] + lnb_ref[...]
    bonus = head_sum(r * k_mod * rk_ref[...]) * v
    o_ref[0] = ((y + bonus) * g).astype(o_ref.dtype)


def _rwkv7(proj, params, layer, TS=512):
    B, S, _ = proj.shape
    W = RWKV_W
    n_sys = (TS // RWKV_CHUNK) * RWKV_HEADS
    per_layer = lambda a: pl.BlockSpec((None,) + a.shape[1:], lambda b, s: (layer, 0, 0))
    per_sys = lambda: pltpu.VMEM((n_sys, RWKV_CHUNK, HEAD_DIM), F32)
    return pl.pallas_call(
        functools.partial(_rwkv_kernel, TS=TS),
        out_shape=jax.ShapeDtypeStruct((B, S, W), BF16),
        grid=(B, S // TS),
        in_specs=[
            pl.BlockSpec((1, TS, RWKV_IN), lambda b, s: (b, s, COL_RWKV // RWKV_IN)),
        ] + [per_layer(a) for a in params],
        out_specs=pl.BlockSpec((1, TS, W), lambda b, s: (b, s, 0)),
        scratch_shapes=[
            pltpu.VMEM((RWKV_HEADS, HEAD_DIM, HEAD_DIM), F32),
            pltpu.VMEM((1, RWKV_IN), F32),
            pltpu.VMEM((n_sys, 2 * RWKV_CHUNK, HEAD_DIM), F32),
            per_sys(), per_sys(), per_sys(), per_sys(),
            pltpu.VMEM((TS, W), F32),
        ],
        compiler_params=pltpu.CompilerParams(
            dimension_semantics=("parallel", "arbitrary"), vmem_limit_bytes=VMEM_LIMIT),
        name="rwkv7",
    )(proj, *params)


def _post_kernel(h_ref, oret_ref, osb_ref, orw_ref, p_ref, wo_ref, gm_ref, w1_ref, w2_ref,
                 gp_ref, wpg_ref, wpe_ref, gf_ref, out_ref, *, ff_chunk, final):
    h = h_ref[...]
    h = h + jnp.dot(oret_ref[...], wo_ref[0:RET_W, :], preferred_element_type=F32)
    h = h + jnp.dot(osb_ref[...], wo_ref[RET_W:RET_W + SB_W, :], preferred_element_type=F32)
    h = h + jnp.dot(orw_ref[...], wo_ref[RET_W + SB_W:, :], preferred_element_type=F32)

    n = _rms(h, gm_ref[...]).astype(BF16)
    out_ref[...] = h
    for c in range(D_FF // ff_chunk):
        cols = slice(c * ff_chunk, (c + 1) * ff_chunk)
        u = jnp.dot(n, w1_ref[:, cols], preferred_element_type=F32)
        u = jnp.square(jnp.maximum(u, 0.0)).astype(BF16)
        out_ref[...] += jnp.dot(u, w2_ref[cols, :], preferred_element_type=F32)
    h = out_ref[...]

    n = _rms(h, gp_ref[...]).astype(BF16)
    gate = _sigmoid(jnp.dot(n, wpg_ref[...], preferred_element_type=F32))
    pe = jnp.dot(p_ref[...].astype(BF16), wpe_ref[...], preferred_element_type=F32)
    h = h + gate * pe
    out_ref[...] = _rms(h, gf_ref[...]) if final else h


def _post(h, o_ret, o_sb, o_rw, p, w_o, g_mlp, w1, w2, g_ple, w_pg, w_pe, g_final, layer, final, tm=512, ff_chunk=1024):
    M, D = h.shape
    rows = lambda n: pl.BlockSpec((tm, n), lambda i: (i, 0))
    const = lambda a: pl.BlockSpec((None,) + a.shape[1:], lambda i: (layer, 0, 0), pipeline_mode=pl.Buffered(1))
    return pl.pallas_call(
        functools.partial(_post_kernel, ff_chunk=ff_chunk, final=final),
        out_shape=jax.ShapeDtypeStruct((M, D), F32),
        grid=(M // tm,),
        in_specs=[
            rows(D), rows(RET_W), rows(SB_W), rows(RWKV_W),
            pl.BlockSpec((None, tm, PLE_DIM), lambda i: (layer, i, 0)),
            const(w_o), const(g_mlp), const(w1), const(w2), const(g_ple), const(w_pg), const(w_pe),
            pl.BlockSpec((1, D), lambda i: (0, 0)),
        ],
        out_specs=rows(D),
        compiler_params=pltpu.CompilerParams(
            dimension_semantics=("parallel",), vmem_limit_bytes=VMEM_LIMIT),
        name="post",
    )(h, o_ret, o_sb, o_rw, p, w_o, g_mlp, w1, w2, g_ple, w_pg, w_pe, g_final)


def _permute_w_in(w_in):
    L, D, _ = w_in.shape
    half = HEAD_DIM // 2

    def pair_lanes(w):
        w = w.reshape(L, D, RET_HEADS // 2, 2, 2, half)
        return jnp.swapaxes(w, 3, 4).reshape(L, D, RET_W)

    sb0 = RET_IN
    return jnp.concatenate([
        pair_lanes(w_in[:, :, 0:RET_W]), pair_lanes(w_in[:, :, RET_W:2 * RET_W]), w_in[:, :, 2 * RET_W:RET_IN],
        w_in[:, :, RET_IN + SB_IN:],
        w_in[:, :, sb0 + SB_W:sb0 + 3 * SB_W], w_in[:, :, sb0:sb0 + SB_W]], axis=2)


def _rope_tables(S):
    pos = jnp.arange(S, dtype=F32)
    inv_freq = 1.0 / (ROPE_BASE ** jnp.linspace(0.0, 1.0, HEAD_DIM // 2, dtype=F32))
    ang = pos[:, None] * inv_freq[None, :]
    cos, sin = jnp.cos(ang), jnp.sin(ang)
    return jnp.concatenate([cos, cos, cos, cos], axis=-1), jnp.concatenate([-sin, -sin, sin, sin], axis=-1)


def kernel(x, p, norm_mix_g, norm_mlp_g, norm_ple_g, w_in, ret_norm_g, rwkv_mu, rwkv_w0, rwkv_w_up, rwkv_a0, rwkv_a_up, rwkv_g_up, rwkv_k_k, rwkv_k_a, rwkv_r_k, rwkv_ln_w, rwkv_ln_b, w_o, w_mlp_in, w_mlp_out, w_pe, w_pg, final_norm_g):
    B, S, D = x.shape
    depth = w_in.shape[0]
    M = B * S
    w_in_p = _permute_w_in(w_in).astype(BF16)
    w_o_b = w_o.astype(BF16)
    w1_b = w_mlp_in.astype(BF16)
    w2_b = w_mlp_out.astype(BF16)
    w_pg_b = w_pg.astype(BF16)
    w_pe_b = w_pe.astype(BF16)
    cos4, sin4 = _rope_tables(S)

    vec = lambda a: a.reshape(depth, 1, a.shape[-1])
    g_mix, g_mlp, g_ple, g_ret = vec(norm_mix_g), vec(norm_mlp_g), vec(norm_ple_g), vec(ret_norm_g)
    rwkv_params = (vec(rwkv_mu), vec(rwkv_w0), rwkv_w_up, vec(rwkv_a0), rwkv_a_up, rwkv_g_up,
                   vec(rwkv_k_k), vec(rwkv_k_a), vec(rwkv_r_k), vec(rwkv_ln_w), vec(rwkv_ln_b))
    p_rows = p.reshape(depth, M, PLE_DIM)

    h = x.reshape(M, D)
    for i in range(depth):
        proj, proj_sb = _norm_proj(h, g_mix, w_in_p, i)
        proj = proj.reshape(B, S, MAIN_W)
        o_ret = _retention(proj, cos4, sin4, g_ret, i)
        o_sb = _stick_breaking(proj_sb.reshape(B, S, SB_IN))
        o_rw = _rwkv7(proj, rwkv_params, i)
        h = _post(h, o_ret.reshape(M, RET_W), o_sb.reshape(M, SB_W), o_rw.reshape(M, RWKV_W), p_rows,
                  w_o_b, g_mlp, w1_b, w2_b, g_ple, w_pg_b, w_pe_b, final_norm_g.reshape(1, D), i, i == depth - 1)
    return h.reshape(B, S, D)
```

```python
import functools
import math

import numpy as np
import jax
import jax.numpy as jnp
from jax import lax
from jax.experimental import pallas as pl
from jax.experimental.pallas import tpu as pltpu

F32 = jnp.float32
BF16 = jnp.bfloat16

D_MODEL = 1024
HEAD_DIM = 64
RET_W = 512
SB_W = 256
RWKV_W = 256
RET_HEADS = 8
SB_HEADS = 4
RWKV_HEADS = 4
DECAY_LORA = 64
AAA_LORA = 64
GATE_LORA = 128
RWKV_IN = 3 * RWKV_W + DECAY_LORA + AAA_LORA + GATE_LORA
RET_IN = 4 * RET_W
SB_IN = 3 * SB_W
IN_W = RET_IN + SB_IN + RWKV_IN
D_FF = 4 * D_MODEL
PLE_DIM = 256
RET_CHUNK = 64
SB_BLOCK = 256
ROPE_BASE = 10000.0
NORM_EPS = 1e-6
RWKV_GN_EPS = 64e-5

COL_RET = 0
COL_RWKV = RET_IN
MAIN_W = RET_IN + RWKV_IN

RET_LOG_G = tuple(math.log(1.0 - 2.0 ** (-5.0 - h)) for h in range(RET_HEADS))
SB_SKIP_LOG = -104.0
RWKV_CHUNK = 64

VMEM_LIMIT = 56 * 1024 * 1024

NT = (((1,), (1,)), ((), ()))
TN = (((0,), (0,)), ((), ()))


def _mm(a, b, dims=None, exact=False):
    if dims is None:
        dims = (((a.ndim - 1,), (0,)), ((), ()))
    if exact:
        return lax.dot_general(a, b, dims, precision=lax.Precision.HIGHEST, preferred_element_type=F32)
    return lax.dot_general(a.astype(BF16), b.astype(BF16), dims, preferred_element_type=F32)


def _rms(x, g):
    return x * lax.rsqrt(jnp.mean(x * x, axis=-1, keepdims=True) + NORM_EPS) * g


def _softplus(z):
    return jnp.maximum(z, 0.0) + jnp.log(1.0 + jnp.exp(-jnp.abs(z)))


def _sigmoid(z):
    return 1.0 / (1.0 + jnp.exp(-z))


def _norm_proj_kernel(h_ref, g_ref, w_ref, o_ref, sb_ref):
    n = _rms(h_ref[...], g_ref[...]).astype(BF16)
    o_ref[...] = jnp.dot(n, w_ref[:, 0:MAIN_W], preferred_element_type=F32)
    sb_ref[...] = jnp.dot(n, w_ref[:, MAIN_W:], preferred_element_type=F32).astype(BF16)


def _norm_proj(h, g, w, layer, tm=512):
    M, D = h.shape
    N = w.shape[2]
    return pl.pallas_call(
        _norm_proj_kernel,
        out_shape=(jax.ShapeDtypeStruct((M, MAIN_W), F32), jax.ShapeDtypeStruct((M, SB_IN), BF16)),
        grid=(M // tm,),
        in_specs=[
            pl.BlockSpec((tm, D), lambda i: (i, 0)),
            pl.BlockSpec((None, 1, D), lambda i: (layer, 0, 0)),
            pl.BlockSpec((None, D, N), lambda i: (layer, 0, 0)),
        ],
        out_specs=(pl.BlockSpec((tm, MAIN_W), lambda i: (i, 0)), pl.BlockSpec((tm, SB_IN), lambda i: (i, 0))),
        compiler_params=pltpu.CompilerParams(
            dimension_semantics=("parallel",), vmem_limit_bytes=VMEM_LIMIT),
        name="norm_proj",
    )(h, g, w)


def _retention_kernel(q_ref, k_ref, v_ref, g_ref, cos_ref, sin_ref, ng_ref, dec_ref, qdec_ref, kdec_ref,
                      sdec_ref, own_ref, hsum_ref, o_ref, state_ref, *, T):
    @pl.when(pl.program_id(1) == 0)
    def _():
        state_ref[...] = jnp.zeros_like(state_ref)

    cos = cos_ref[...]
    sin = sin_ref[...]
    lane = lax.broadcasted_iota(jnp.int32, (T, 128), 1)
    k_even = (lane % HEAD_DIM) < (HEAD_DIM // 2)
    v_even = lane < HEAD_DIM
    own = own_ref[...]
    hsum = hsum_ref[...]
    dot = functools.partial(jnp.dot, preferred_element_type=F32)

    for j in range(RET_HEADS // 2):
        cols = slice(j * 128, (j + 1) * 128)
        q = q_ref[0, :, cols]
        k = k_ref[0, :, cols]
        q = q * cos + pltpu.roll(q, HEAD_DIM, 1) * sin
        k = (k * cos + pltpu.roll(k, HEAD_DIM, 1) * sin) * (HEAD_DIM ** -0.5)
        v_b = v_ref[0, :, cols].astype(BF16)
        k_b = k.astype(BF16)
        q_b = q.astype(BF16)
        zero = jnp.zeros_like(q_b)
        pv = []
        for e in range(2):
            qm = jnp.where(k_even if e == 0 else ~k_even, q_b, zero)
            scores = lax.dot_general(qm, k_b, NT, preferred_element_type=F32) * dec_ref[2 * j + e]
            pv.append(dot(scores.astype(BF16), v_b))
        st = state_ref[j]
        o = jnp.where(v_even, pv[0], pv[1]) + dot((q * qdec_ref[:, cols]).astype(BF16), st.astype(BF16))
        upd = lax.dot_general((k * kdec_ref[:, cols]).astype(BF16), v_b, TN, preferred_element_type=F32)
        state_ref[j] = st * sdec_ref[j] + upd * own
        ms = dot((o * o).astype(BF16), hsum) * (1.0 / HEAD_DIM)
        gate = g_ref[0, :, cols]
        gate = gate * _sigmoid(gate)
        o_ref[0, :, cols] = (o * lax.rsqrt(ms + NORM_EPS) * ng_ref[:, cols] * gate).astype(o_ref.dtype)


def _retention_tables(T):
    log_g = jnp.log(1.0 - jnp.exp2(-5.0 - jnp.arange(RET_HEADS, dtype=F32)))
    n = jnp.arange(T)
    visible = (n[None, :] <= n[:, None]) | ((n[None, :] // RET_CHUNK) == (n[:, None] // RET_CHUNK))
    dist = jnp.abs(n[:, None] - n[None, :]).astype(F32)
    dec = jnp.where(visible[None], jnp.exp(dist[None] * log_g[:, None, None]), 0.0)
    lane = np.arange(RET_W)
    k_head = 2 * (lane // 128) + ((lane % HEAD_DIM) // (HEAD_DIM // 2))
    v_head = lane // HEAD_DIM
    pos = jnp.arange(T, dtype=F32)[:, None]
    qdec = jnp.exp((pos + 1.0) * log_g[k_head][None, :])
    kdec = jnp.exp((T - 1.0 - pos) * log_g[k_head][None, :])
    block_decay = jnp.exp(T * log_g)
    sdec = jnp.broadcast_to(block_decay[k_head].reshape(RET_HEADS // 2, 128, 1), (RET_HEADS // 2, 128, 128))
    own = jnp.asarray(k_head[:128, None] == v_head[None, :128], F32)
    hsum = jnp.asarray(v_head[:128, None] == v_head[None, :128], BF16)
    return dec, qdec, kdec, sdec, own, hsum


def _retention(proj, cos4, sin4, norm_g, layer, T=256):
    B, S, _ = proj.shape
    dec, qdec, kdec, sdec, own, hsum = _retention_tables(T)
    blk = lambda c: pl.BlockSpec((1, T, RET_W), lambda b, s, c=c: (b, s, c))
    const = lambda a: pl.BlockSpec(a.shape, lambda b, s: (0,) * a.ndim)
    return pl.pallas_call(
        functools.partial(_retention_kernel, T=T),
        out_shape=jax.ShapeDtypeStruct((B, S, RET_W), BF16),
        grid=(B, S // T),
        in_specs=[
            blk(0), blk(1), blk(2), blk(3),
            pl.BlockSpec((T, 128), lambda b, s: (s, 0)),
            pl.BlockSpec((T, 128), lambda b, s: (s, 0)),
            pl.BlockSpec((None, 1, RET_W), lambda b, s: (layer, 0, 0)),
            const(dec), const(qdec), const(kdec), const(sdec), const(own), const(hsum),
        ],
        out_specs=pl.BlockSpec((1, T, RET_W), lambda b, s: (b, s, 0)),
        scratch_shapes=[pltpu.VMEM((RET_HEADS // 2, 128, 128), F32)],
        compiler_params=pltpu.CompilerParams(
            dimension_semantics=("parallel", "arbitrary"), vmem_limit_bytes=VMEM_LIMIT),
        name="retention",
    )(proj, proj, proj, proj, cos4, sin4, norm_g, dec, qdec, kdec, sdec, own, hsum)


def _stick_kernel(q_ref, kv_ref, o_ref, *, TB):
    i = pl.program_id(1)
    row = lax.broadcasted_iota(jnp.int32, (TB, TB), 0)
    col = lax.broadcasted_iota(jnp.int32, (TB, TB), 1)
    strict = col < row
    rev_incl = (row >= col).astype(BF16)
    low_lanes = lax.broadcasted_iota(jnp.int32, (TB, 2 * HEAD_DIM), 1) < HEAD_DIM

    qm = []
    for h in range(SB_HEADS):
        qp = q_ref[0, :, (h // 2) * 128:(h // 2 + 1) * 128] * (HEAD_DIM ** -0.5)
        qm.append(jnp.where(low_lanes if h % 2 == 0 else ~low_lanes, qp, jnp.zeros_like(qp)))

    def block(n, o, acc, diagonal):
        rows = pl.ds(pl.multiple_of(n * TB, TB), TB)
        heads = range(SB_HEADS)
        kp = [kv_ref[0, rows, j * 128:(j + 1) * 128] for j in range(SB_HEADS // 2)]
        vp = [kv_ref[0, rows, SB_W + j * 128:SB_W + (j + 1) * 128] for j in range(SB_HEADS // 2)]
        z = [lax.dot_general(qm[h], kp[h // 2], NT, preferred_element_type=F32) for h in heads]
        ls = [-_softplus(z[h]) for h in heads]
        if diagonal:
            ls = [jnp.where(strict, ls[h], 0.0) for h in heads]
        cum = [jnp.dot(ls[h].astype(BF16), rev_incl, preferred_element_type=F32) for h in heads]
        if acc is None:
            w = [jnp.exp(z[h] + cum[h]) for h in heads]
        else:
            w = [jnp.exp(z[h] + cum[h] + acc[h]) for h in heads]
        if diagonal:
            w = [jnp.where(strict, w[h], 0.0) for h in heads]
        pv = [jnp.dot(w[h].astype(BF16), vp[h // 2], preferred_element_type=F32) for h in heads]
        tot = [jnp.sum(ls[h], axis=-1, keepdims=True) for h in heads]
        new_acc = tot if acc is None else [acc[h] + tot[h] for h in heads]
        upd = [jnp.where(low_lanes, pv[2 * j], pv[2 * j + 1]) for j in range(SB_HEADS // 2)]
        new_o = upd if o is None else [o[j] + upd[j] for j in range(SB_HEADS // 2)]
        return tuple(new_o), tuple(new_acc)

    def top_of(acc):
        return functools.reduce(jnp.maximum, [jnp.max(a) for a in acc])

    o, acc = block(i, None, None, True)

    def cond(c):
        n, _, _, top = c
        return jnp.logical_and(n >= 0, top > SB_SKIP_LOG)

    def body(c):
        n, o, acc, _ = c
        o, acc = block(n, o, acc, False)
        return n - 1, o, acc, top_of(acc)

    _, o, _, _ = lax.while_loop(cond, body, (i - 1, o, acc, top_of(acc)))
    for j in range(SB_HEADS // 2):
        o_ref[0, :, j * 128:(j + 1) * 128] = o[j].astype(o_ref.dtype)


def _stick_breaking(proj_sb, TB=SB_BLOCK):
    B, S, _ = proj_sb.shape
    return pl.pallas_call(
        functools.partial(_stick_kernel, TB=TB),
        out_shape=jax.ShapeDtypeStruct((B, S, SB_W), BF16),
        grid=(B, S // TB),
        in_specs=[
            pl.BlockSpec((1, TB, SB_W), lambda b, i: (b, i, 2)),
            pl.BlockSpec((1, S, 2 * SB_W), lambda b, i: (b, 0, 0)),
        ],
        out_specs=pl.BlockSpec((1, TB, SB_W), lambda b, i: (b, i, 0)),
        compiler_params=pltpu.CompilerParams(
            dimension_semantics=("parallel", "arbitrary"), vmem_limit_bytes=VMEM_LIMIT),
        name="stick_breaking",
    )(proj_sb, proj_sb)


def _rwkv_kernel(z_ref, mu_ref, w0_ref, wup_ref, a0_ref, aup_ref, gup_ref, kk_ref, ka_ref, rk_ref,
                 lnw_ref, lnb_ref, o_ref,
                 state_ref, prev_ref, lhs_s, solv_s, mrb_s, yfix_s, sfix_s, y_s, *, TS):
    C = RWKV_CHUNK
    W = RWKV_W
    NC = TS // C

    @pl.when(pl.program_id(1) == 0)
    def _():
        state_ref[...] = jnp.zeros_like(state_ref)
        prev_ref[...] = jnp.zeros_like(prev_ref)

    z = z_ref[0]
    first_row = lax.broadcasted_iota(jnp.int32, (TS, 1), 0) == 0
    z_prev = jnp.where(first_row, prev_ref[...], pltpu.roll(z, 1, 0))
    prev_ref[...] = z[TS - 1:TS, :]
    z = z + mu_ref[...] * (z_prev - z)

    r, k, v = z[:, 0:W], z[:, W:2 * W], z[:, 2 * W:3 * W]
    wd = z[:, 3 * W:3 * W + DECAY_LORA]
    ad = z[:, 3 * W + DECAY_LORA:3 * W + DECAY_LORA + AAA_LORA]
    gd = z[:, 3 * W + DECAY_LORA + AAA_LORA:]

    gi = lax.broadcasted_iota(jnp.int32, (W, W), 0) // HEAD_DIM
    gj = lax.broadcasted_iota(jnp.int32, (W, W), 1) // HEAD_DIM
    same_head = (gi == gj).astype(BF16)

    def head_sum(x):
        return jnp.dot(x.astype(BF16), same_head, preferred_element_type=F32)

    log_w = -_softplus(-(w0_ref[...] + _mm(jnp.tanh(wd), wup_ref[...]))) - 0.5
    lw = -jnp.exp(log_w)
    a = _sigmoid(a0_ref[...] + _mm(ad, aup_ref[...]))
    g = _mm(_sigmoid(gd), gup_ref[...])
    kk = k * kk_ref[...]
    kk = kk * lax.rsqrt(jnp.maximum(head_sum(kk * kk), 1e-24))
    k_mod = k * (1.0 + (a - 1.0) * ka_ref[...])
    a_vec = -kk
    b_vec = kk * a

    ti = lax.broadcasted_iota(jnp.int32, (2 * C, C), 0)
    tj = lax.broadcasted_iota(jnp.int32, (2 * C, C), 1)
    stacked_tri = tj <= jnp.where(ti < C, ti - 1, ti - C)
    tri_incl = (lax.broadcasted_iota(jnp.int32, (C, C), 1) <= lax.broadcasted_iota(jnp.int32, (C, C), 0)).astype(BF16)

    systems = [(c, h) for c in range(NC) for h in range(RWKV_HEADS)]
    head = lambda t, h: t[:, h * HEAD_DIM:(h + 1) * HEAD_DIM]
    dot = functools.partial(jnp.dot, preferred_element_type=F32)
    dot_nt = functools.partial(lax.dot_general, dimension_numbers=NT, preferred_element_type=F32)
    dot_tn = functools.partial(lax.dot_general, dimension_numbers=TN, preferred_element_type=F32)

    at_f, at_b, rt_b, bt_b, kt_b, bh_b, kh_b, v_b, p_end = [], [], [], [], [], [], [], [], []
    for c in range(NC):
        rows = slice(c * C, (c + 1) * C)
        lw_c = lw[rows]
        lw_hi = lw_c.astype(BF16)
        lw_lo = (lw_c - lw_hi.astype(F32)).astype(BF16)
        cs2 = dot(tri_incl, jnp.concatenate([lw_hi, lw_lo], axis=1))
        cs = cs2[:, 0:W] + cs2[:, W:]
        e_incl = jnp.exp(cs)
        e_inv = jnp.exp(-cs)
        p_c = e_incl[C - 1:C, :]
        at = a_vec[rows] * jnp.exp(cs - lw_c)
        bt = b_vec[rows] * e_inv
        kt = k_mod[rows] * e_inv
        at_f.append(at)
        at_b.append(at.astype(BF16))
        rt_b.append((r[rows] * e_incl).astype(BF16))
        bt_b.append(bt.astype(BF16))
        kt_b.append(kt.astype(BF16))
        bh_b.append((bt * p_c).astype(BF16))
        kh_b.append((kt * p_c).astype(BF16))
        v_b.append(v[rows].astype(BF16))
        p_end.append(p_c)

    n_sys = len(systems)
    vh = [head(v_b[c], h) for c, h in systems]
    rth = [head(rt_b[c], h) for c, h in systems]
    gram = [dot_nt(jnp.concatenate([head(at_b[c], h), rth[i]], axis=0),
                   jnp.concatenate([head(bt_b[c], h), head(kt_b[c], h)], axis=0))
            for i, (c, h) in enumerate(systems)]
    on_b = [jnp.where(stacked_tri, gm[:, 0:C], 0.0) for gm in gram]
    on_k = [jnp.where(stacked_tri, gm[:, C:2 * C], 0.0).astype(BF16) for gm in gram]
    kv = [dot(on_k[i], vh[i]) for i in range(n_sys)]
    for i in range(n_sys):
        mrb_s[i] = on_b[i][C:2 * C]
        yfix_s[i] = kv[i][C:2 * C]
    ri = lax.broadcasted_iota(jnp.int32, (C, C), 0)
    ci = lax.broadcasted_iota(jnp.int32, (C, C), 1)
    levels = int(math.log2(C))

    def lower_left(lv):
        same_pair = (ri >> (lv + 1)) == (ci >> (lv + 1))
        return same_pair & (((ri >> lv) & 1) == 1) & (((ci >> lv) & 1) == 0)

    eye = (ri == ci).astype(F32)
    a_low = [m[0:C] for m in on_b]
    inv = [eye + jnp.where(lower_left(0), a, 0.0) for a in a_low]
    for lv in range(1, levels):
        mask = lower_left(lv)
        xd = [dot(jnp.where(mask, a_low[i], 0.0).astype(BF16), inv[i].astype(BF16)) for i in range(n_sys)]
        inv = [inv[i] + dot(inv[i].astype(BF16), xd[i].astype(BF16)) for i in range(n_sys)]
    rhs = [jnp.concatenate([head(at_f[c], h), kv[i][0:C]], axis=1) for i, (c, h) in enumerate(systems)]
    split = lambda t: (t.astype(BF16), (t - t.astype(BF16).astype(F32)).astype(BF16))
    n_hi, n_lo = zip(*[split(inv[i] - eye) for i in range(n_sys)])
    r_hi, r_lo = zip(*[split(rhs[i]) for i in range(n_sys)])
    on_hi = [dot(n_hi[i], jnp.concatenate([r_hi[i], r_lo[i]], axis=1)) for i in range(n_sys)]
    on_lo = [dot(n_lo[i], r_hi[i]) for i in range(n_sys)]
    sol = [rhs[i] + (on_hi[i][:, 0:2 * HEAD_DIM] + on_hi[i][:, 2 * HEAD_DIM:]) + on_lo[i] for i in range(n_sys)]
    for i, (c, h) in enumerate(systems):
        lhs_s[i, 0:C, :] = sol[i][:, 0:HEAD_DIM]
        lhs_s[i, C:2 * C, :] = rth[i].astype(F32)
        solv_s[i] = sol[i][:, HEAD_DIM:]
        sfix_s[i] = dot_tn(vh[i], head(kh_b[c], h))

    heads = range(RWKV_HEADS)
    for c in range(NC):
        rows = slice(c * C, (c + 1) * C)
        base = c * RWKV_HEADS
        st = [state_ref[h] for h in heads]
        on_state = [dot_nt(lhs_s[base + h].astype(BF16), st[h].astype(BF16)) for h in heads]
        u_b = [(on_state[h][0:C] + solv_s[base + h]).astype(BF16) for h in heads]
        yu = [dot(mrb_s[base + h].astype(BF16), u_b[h]) for h in heads]
        su = [dot_tn(u_b[h], head(bh_b[c], h)) for h in heads]
        for h in heads:
            y_s[rows, h * HEAD_DIM:(h + 1) * HEAD_DIM] = on_state[h][C:2 * C] + yu[h] + yfix_s[base + h]
            state_ref[h] = st[h] * head(p_end[c], h) + su[h] + sfix_s[base + h]

    y = y_s[...]
    mean = head_sum(y) * (1.0 / HEAD_DIM)
    d = y - mean
    var = head_sum(d * d) * (1.0 / HEAD_DIM)
    y = d * lax.rsqrt(var + RWKV_GN_EPS) * lnw_ref[...] + lnb_ref[...]
    bonus = head_sum(r * k_mod * rk_ref[...]) * v
    o_ref[0] = ((y + bonus) * g).astype(o_ref.dtype)


def _rwkv7(proj, params, layer, TS=512):
    B, S, _ = proj.shape
    W = RWKV_W
    n_sys = (TS // RWKV_CHUNK) * RWKV_HEADS
    per_layer = lambda a: pl.BlockSpec((None,) + a.shape[1:], lambda b, s: (layer, 0, 0))
    per_sys = lambda: pltpu.VMEM((n_sys, RWKV_CHUNK, HEAD_DIM), F32)
    return pl.pallas_call(
        functools.partial(_rwkv_kernel, TS=TS),
        out_shape=jax.ShapeDtypeStruct((B, S, W), BF16),
        grid=(B, S // TS),
        in_specs=[
            pl.BlockSpec((1, TS, RWKV_IN), lambda b, s: (b, s, COL_RWKV // RWKV_IN)),
        ] + [per_layer(a) for a in params],
        out_specs=pl.BlockSpec((1, TS, W), lambda b, s: (b, s, 0)),
        scratch_shapes=[
            pltpu.VMEM((RWKV_HEADS, HEAD_DIM, HEAD_DIM), F32),
            pltpu.VMEM((1, RWKV_IN), F32),
            pltpu.VMEM((n_sys, 2 * RWKV_CHUNK, HEAD_DIM), F32),
            per_sys(), per_sys(), per_sys(), per_sys(),
            pltpu.VMEM((TS, W), F32),
        ],
        compiler_params=pltpu.CompilerParams(
            dimension_semantics=("parallel", "arbitrary"), vmem_limit_bytes=VMEM_LIMIT),
        name="rwkv7",
    )(proj, *params)


def _post_kernel(h_ref, oret_ref, osb_ref, orw_ref, p_ref, wo_ref, gm_ref, w1_ref, w2_ref,
                 gp_ref, wpg_ref, wpe_ref, gf_ref, out_ref, *, ff_chunk, final):
    h = h_ref[...]
    h = h + jnp.dot(oret_ref[...], wo_ref[0:RET_W, :], preferred_element_type=F32)
    h = h + jnp.dot(osb_ref[...], wo_ref[RET_W:RET_W + SB_W, :], preferred_element_type=F32)
    h = h + jnp.dot(orw_ref[...], wo_ref[RET_W + SB_W:, :], preferred_element_type=F32)

    n = _rms(h, gm_ref[...]).astype(BF16)
    out_ref[...] = h
    for c in range(D_FF // ff_chunk):
        cols = slice(c * ff_chunk, (c + 1) * ff_chunk)
        u = jnp.dot(n, w1_ref[:, cols], preferred_element_type=F32)
        u = jnp.square(jnp.maximum(u, 0.0)).astype(BF16)
        out_ref[...] += jnp.dot(u, w2_ref[cols, :], preferred_element_type=F32)
    h = out_ref[...]

    n = _rms(h, gp_ref[...]).astype(BF16)
    gate = _sigmoid(jnp.dot(n, wpg_ref[...], preferred_element_type=F32))
    pe = jnp.dot(p_ref[...].astype(BF16), wpe_ref[...], preferred_element_type=F32)
    h = h + gate * pe
    out_ref[...] = _rms(h, gf_ref[...]) if final else h


def _post(h, o_ret, o_sb, o_rw, p, w_o, g_mlp, w1, w2, g_ple, w_pg, w_pe, g_final, layer, final, tm=512, ff_chunk=1024):
    M, D = h.shape
    rows = lambda n: pl.BlockSpec((tm, n), lambda i: (i, 0))
    const = lambda a: pl.BlockSpec((None,) + a.shape[1:], lambda i: (layer, 0, 0), pipeline_mode=pl.Buffered(1))
    return pl.pallas_call(
        functools.partial(_post_kernel, ff_chunk=ff_chunk, final=final),
        out_shape=jax.ShapeDtypeStruct((M, D), F32),
        grid=(M // tm,),
        in_specs=[
            rows(D), rows(RET_W), rows(SB_W), rows(RWKV_W),
            pl.BlockSpec((None, tm, PLE_DIM), lambda i: (layer, i, 0)),
            const(w_o), const(g_mlp), const(w1), const(w2), const(g_ple), const(w_pg), const(w_pe),
            pl.BlockSpec((1, D), lambda i: (0, 0)),
        ],
        out_specs=rows(D),
        compiler_params=pltpu.CompilerParams(
            dimension_semantics=("parallel",), vmem_limit_bytes=VMEM_LIMIT),
        name="post",
    )(h, o_ret, o_sb, o_rw, p, w_o, g_mlp, w1, w2, g_ple, w_pg, w_pe, g_final)


def _permute_w_in(w_in):
    L, D, _ = w_in.shape
    half = HEAD_DIM // 2

    def pair_lanes(w):
        w = w.reshape(L, D, RET_HEADS // 2, 2, 2, half)
        return jnp.swapaxes(w, 3, 4).reshape(L, D, RET_W)

    sb0 = RET_IN
    return jnp.concatenate([
        pair_lanes(w_in[:, :, 0:RET_W]), pair_lanes(w_in[:, :, RET_W:2 * RET_W]), w_in[:, :, 2 * RET_W:RET_IN],
        w_in[:, :, RET_IN + SB_IN:],
        w_in[:, :, sb0 + SB_W:sb0 + 3 * SB_W], w_in[:, :, sb0:sb0 + SB_W]], axis=2)


def _rope_tables(S):
    pos = jnp.arange(S, dtype=F32)
    inv_freq = 1.0 / (ROPE_BASE ** jnp.linspace(0.0, 1.0, HEAD_DIM // 2, dtype=F32))
    ang = pos[:, None] * inv_freq[None, :]
    cos, sin = jnp.cos(ang), jnp.sin(ang)
    return jnp.concatenate([cos, cos, cos, cos], axis=-1), jnp.concatenate([-sin, -sin, sin, sin], axis=-1)


def kernel(x, p, norm_mix_g, norm_mlp_g, norm_ple_g, w_in, ret_norm_g, rwkv_mu, rwkv_w0, rwkv_w_up, rwkv_a0, rwkv_a_up, rwkv_g_up, rwkv_k_k, rwkv_k_a, rwkv_r_k, rwkv_ln_w, rwkv_ln_b, w_o, w_mlp_in, w_mlp_out, w_pe, w_pg, final_norm_g):
    B, S, D = x.shape
    depth = w_in.shape[0]
    M = B * S
    w_in_p = _permute_w_in(w_in).astype(BF16)
    w_o_b = w_o.astype(BF16)
    w1_b = w_mlp_in.astype(BF16)
    w2_b = w_mlp_out.astype(BF16)
    w_pg_b = w_pg.astype(BF16)
    w_pe_b = w_pe.astype(BF16)
    cos4, sin4 = _rope_tables(S)

    vec = lambda a: a.reshape(depth, 1, a.shape[-1])
    g_mix, g_mlp, g_ple, g_ret = vec(norm_mix_g), vec(norm_mlp_g), vec(norm_ple_g), vec(ret_norm_g)
    rwkv_params = (vec(rwkv_mu), vec(rwkv_w0), rwkv_w_up, vec(rwkv_a0), rwkv_a_up, rwkv_g_up,
                   vec(rwkv_k_k), vec(rwkv_k_a), vec(rwkv_r_k), vec(rwkv_ln_w), vec(rwkv_ln_b))
    p_rows = p.reshape(depth, M, PLE_DIM)

    h = x.reshape(M, D)
    for i in range(depth):
        proj, proj_sb = _norm_proj(h, g_mix, w_in_p, i)
        proj = proj.reshape(B, S, MAIN_W)
        o_ret = _retention(proj, cos4, sin4, g_ret, i)
        o_sb = _stick_breaking(proj_sb.reshape(B, S, SB_IN))
        o_rw = _rwkv7(proj, rwkv_params, i)
        h = _post(h, o_ret.reshape(M, RET_W), o_sb.reshape(M, SB_W), o_rw.reshape(M, RWKV_W), p_rows,
                  w_o_b, g_mlp, w1_b, w2_b, g_ple, w_pg_b, w_pe_b, final_norm_g.reshape(1, D), i, i == depth - 1)
    return h.reshape(B, S, D)
```
